```python
import math
import jax, jax.numpy as jnp
from jax import lax
import numpy as np

D_MODEL = 1024
BATCH = 8
SEQ = 4096
DEPTH = 4
DEC_BATCH = 4
DEC_SEQ = 4096
PAST_LEN = 128

DA_HEADS = 4
DA_HEAD_DIM = 64
DA_WIDTH = DA_HEADS * 2 * DA_HEAD_DIM
Q_BLOCK = 128
ROPE_THETA = 10000.0
CONV_WIDTH = 512
CONV_KERNEL = 31
GLA_HEADS = 4
GLA_DK = 64
GLA_DV = 128
GLA_KEY_WIDTH = GLA_HEADS * GLA_DK
GLA_VAL_WIDTH = GLA_HEADS * GLA_DV
GLA_GATE_RANK = 16
GLA_TAU = 16.0
GLA_CHUNK = 64
N_BRANCH = 3
N_EXPERTS = 16
EXPERT_FF = 2048
CAPACITY_FACTOR = 2
DEEPNORM_ALPHA = (2 * DEPTH) ** 0.25
DEEPNORM_BETA = (8 * DEPTH) ** -0.25
NORM_EPS = 1e-5

SPLIT_SIZES = (DA_WIDTH, DA_WIDTH, DA_WIDTH,
               2 * CONV_WIDTH,
               GLA_KEY_WIDTH, GLA_KEY_WIDTH,
               GLA_VAL_WIDTH, GLA_VAL_WIDTH,
               GLA_GATE_RANK, GLA_GATE_RANK,
               N_BRANCH * D_MODEL)
IN_WIDTH = sum(SPLIT_SIZES)
SPLIT_POINTS = tuple(int(v) for v in np.cumsum(SPLIT_SIZES)[:-1])

kernel_name = "hybrid_diffattn_conv_gla_ecmoe_encoder"


def layernorm(x, g, b):
    xf = x.astype(jnp.float32)
    mu = jnp.mean(xf, axis=-1, keepdims=True)
    var = jnp.mean(jnp.square(xf - mu), axis=-1, keepdims=True)
    return ((xf - mu) * lax.rsqrt(var + NORM_EPS) * g.astype(jnp.float32) + b.astype(jnp.float32)).astype(x.dtype)


def rmsnorm(x, g):
    xf = x.astype(jnp.float32)
    y = xf * lax.rsqrt(jnp.mean(jnp.square(xf), axis=-1, keepdims=True) + NORM_EPS)
    return (y * g.astype(jnp.float32)).astype(x.dtype)


def rope(x, pos):
    d = x.shape[-1]
    inv = 1.0 / (ROPE_THETA ** (jnp.arange(0, d, 2, dtype=jnp.float32) / d))
    ang = pos[:, None] * inv[None, :]
    c = jnp.cos(ang)[None, :, None, None, :]
    s = jnp.sin(ang)[None, :, None, None, :]
    xf = x.astype(jnp.float32)
    x1, x2 = xf[..., : d // 2], xf[..., d // 2:]
    return jnp.concatenate([x1 * c - x2 * s, x1 * s + x2 * c], axis=-1).astype(x.dtype)


def diff_attention(q, k, v, lam, lam_init, subln_g, w_o):
    B, S, _ = q.shape
    H, d = DA_HEADS, DA_HEAD_DIM
    pos = jnp.arange(S, dtype=jnp.float32)
    q = rope(q.reshape(B, S, H, 2, d), pos) * (d ** -0.5)
    k = rope(k.reshape(B, S, H, 2, d), pos)
    v = v.reshape(B, S, H, 2 * d)
    nb = S // Q_BLOCK
    qb = q.reshape(B, nb, Q_BLOCK, H, 2, d).transpose(1, 0, 2, 3, 4, 5)

    def block(qi):
        s = jnp.einsum('bqhcd,bkhcd->bhcqk', qi, k, preferred_element_type=jnp.float32)
        p = jax.nn.softmax(s, axis=-1)
        pd = p[:, :, 0] - lam * p[:, :, 1]
        return jnp.einsum('bhqk,bkhe->bqhe', pd.astype(v.dtype), v)

    o = lax.map(block, qb)
    o = o.transpose(1, 0, 2, 3, 4).reshape(B, S, H, 2 * d)
    o = rmsnorm(o, subln_g) * (1.0 - lam_init)
    return o.reshape(B, S, DA_WIDTH) @ w_o


def conformer_conv(u2, conv_w, conv_b, ln_g, ln_b, w_o):
    u = u2[..., :CONV_WIDTH] * jax.nn.sigmoid(u2[..., CONV_WIDTH:])
    pad = CONV_KERNEL // 2
    h = lax.conv_general_dilated(u, conv_w[:, None, :].astype(u.dtype), window_strides=(1,),
                                 padding=[(pad, pad)], dimension_numbers=('NWC', 'WIO', 'NWC'),
                                 feature_group_count=CONV_WIDTH) + conv_b
    h = layernorm(h, ln_g, ln_b)
    return jax.nn.silu(h) @ w_o


def gla_direction(q, k, v, g):
    B, S, H, dk = q.shape
    dv = v.shape[-1]
    C = GLA_CHUNK
    n = S // C
    to_chunks = lambda t: t.reshape(B, n, C, H, t.shape[-1]).transpose(1, 0, 3, 2, 4)
    q, k, v, g = to_chunks(q), to_chunks(k), to_chunks(v), to_chunks(g)
    b = jnp.cumsum(g, axis=3)
    b_last = b[:, :, :, -1:, :]
    qt = q * jnp.exp(b)
    kt = k * jnp.exp(-b)
    kd = k * jnp.exp(b_last - b)
    decay = jnp.exp(b_last[:, :, :, 0, :])
    mask = jnp.tril(jnp.ones((C, C), dtype=bool))
    A = jnp.where(mask, jnp.einsum('nbhid,nbhjd->nbhij', qt, kt), 0.0)
    o_intra = jnp.einsum('nbhij,nbhje->nbhie', A, v)

    def step(state, inp):
        qc, kc, vc, dc = inp
        o = jnp.einsum('bhid,bhde->bhie', qc, state)
        state = dc[..., None] * state + jnp.einsum('bhjd,bhje->bhde', kc, vc)
        return state, o

    state0 = jnp.zeros((B, H, dk, dv), jnp.float32)
    _, o_inter = lax.scan(step, state0, (qt, kd, v, decay))
    o = o_intra + o_inter
    return o.transpose(1, 0, 3, 2, 4).reshape(B, S, H, dv)


def bi_gla(q, k, v, r, lr_f, lr_b, gate_w2, gate_b, norm_g, w_o):
    B, S, _ = q.shape
    dt = q.dtype
    f32 = jnp.float32
    qf = q.astype(f32).reshape(B, S, GLA_HEADS, GLA_DK) * (GLA_DK ** -0.5)
    kf = k.astype(f32).reshape(B, S, GLA_HEADS, GLA_DK)
    vf = v.astype(f32).reshape(B, S, GLA_HEADS, GLA_DV)
    gw = gate_w2.astype(f32)
    gbias = gate_b.astype(f32)
    gf = (jax.nn.log_sigmoid(lr_f.astype(f32) @ gw[0] + gbias[0]) / GLA_TAU).reshape(B, S, GLA_HEADS, GLA_DK)
    gb = (jax.nn.log_sigmoid(lr_b.astype(f32) @ gw[1] + gbias[1]) / GLA_TAU).reshape(B, S, GLA_HEADS, GLA_DK)
    o_f = gla_direction(qf, kf, vf, gf)
    flip = lambda t: jnp.flip(t, axis=1)
    o_b = flip(gla_direction(flip(qf), flip(kf), flip(vf), flip(gb)))
    o = rmsnorm(o_f + o_b, norm_g).reshape(B, S, GLA_VAL_WIDTH)
    o = o * jax.nn.silu(r.astype(f32))
    return o.astype(dt) @ w_o


def expert_choice_ffn(x, w_router, w_gate, w_up, w_down):
    B, S, D = x.shape
    N = B * S
    cap = CAPACITY_FACTOR * N // N_EXPERTS
    xt = x.reshape(N, D)
    aff = jax.nn.softmax((xt @ w_router).astype(jnp.float32), axis=-1)
    G, I = lax.top_k(aff.T, cap)
    xe = xt[I]
    h = jax.nn.silu(jnp.einsum('ecd,edf->ecf', xe, w_gate)) * jnp.einsum('ecd,edf->ecf', xe, w_up)
    ye = jnp.einsum('ecf,efd->ecd', h, w_down) * G[..., None].astype(x.dtype)
    out = jnp.zeros_like(xt).at[I.reshape(-1)].add(ye.reshape(-1, D))
    return out.reshape(B, S, D)


def encode(x, w_in, da_lam_q1, da_lam_k1, da_lam_q2, da_lam_k2, da_subln_g, da_w_o,
           conv_w, conv_b, conv_ln_g, conv_ln_b, conv_w_o,
           gla_gate_w2, gla_gate_b, gla_norm_g, gla_w_o,
           w_out, ln1_g, ln1_b, w_router, w_gate, w_up, w_down, ln2_g, ln2_b):
    B, S, D = x.shape
    for l in range(DEPTH):
        z = x @ w_in[l]
        (dq, dk_, dv_, cu, gq, gk, gv, gr, glf, glb, mg) = jnp.split(z, SPLIT_POINTS, axis=-1)
        lam_init = 0.8 - 0.6 * math.exp(-0.3 * l)
        lam = (jnp.exp(jnp.sum(da_lam_q1[l].astype(jnp.float32) * da_lam_k1[l].astype(jnp.float32)))
               - jnp.exp(jnp.sum(da_lam_q2[l].astype(jnp.float32) * da_lam_k2[l].astype(jnp.float32)))
               + lam_init)
        a = diff_attention(dq, dk_, dv_, lam, lam_init, da_subln_g[l], da_w_o[l])
        c = conformer_conv(cu, conv_w[l], conv_b[l], conv_ln_g[l], conv_ln_b[l], conv_w_o[l])
        g = bi_gla(gq, gk, gv, gr, glf, glb, gla_gate_w2[l], gla_gate_b[l], gla_norm_g[l], gla_w_o[l])
        gates = jax.nn.sigmoid(mg.reshape(B, S, N_BRANCH, D))
        m = gates[:, :, 0] * a + gates[:, :, 1] * c + gates[:, :, 2] * g
        x = layernorm(DEEPNORM_ALPHA * x + m @ w_out[l], ln1_g[l], ln1_b[l])
        f = expert_choice_ffn(x, w_router[l], w_gate[l], w_up[l], w_down[l])
        x = layernorm(DEEPNORM_ALPHA * x + f, ln2_g[l], ln2_b[l])
    return x


def setup_inputs(seed: int = 0) -> dict:
    key = jax.random.key(seed)
    ks = jax.random.split(key, 32)
    f32 = jnp.float32
    nrm = lambda k, shape, scale: jax.random.normal(k, shape, f32) * scale
    gain = lambda k, shape: 1.0 + 0.02 * jax.random.normal(k, shape, f32)
    bias = lambda k, shape: 0.02 * jax.random.normal(k, shape, f32)
    beta = DEEPNORM_BETA
    return {
        "x_prompt": jax.random.normal(ks[0], (BATCH, SEQ, D_MODEL), f32),
        "x_sample": jax.random.normal(ks[1], (DEC_BATCH, DEC_SEQ, D_MODEL), f32),
        "w_in": nrm(ks[2], (DEPTH, D_MODEL, IN_WIDTH), D_MODEL ** -0.5),
        "da_lam_q1": nrm(ks[3], (DEPTH, DA_HEAD_DIM), 0.1),
        "da_lam_k1": nrm(ks[4], (DEPTH, DA_HEAD_DIM), 0.1),
        "da_lam_q2": nrm(ks[5], (DEPTH, DA_HEAD_DIM), 0.1),
        "da_lam_k2": nrm(ks[6], (DEPTH, DA_HEAD_DIM), 0.1),
        "da_subln_g": gain(ks[7], (DEPTH, 2 * DA_HEAD_DIM)),
        "da_w_o": nrm(ks[8], (DEPTH, DA_WIDTH, D_MODEL), beta * DA_WIDTH ** -0.5),
        "conv_w": nrm(ks[9], (DEPTH, CONV_KERNEL, CONV_WIDTH), CONV_KERNEL ** -0.5),
        "conv_b": bias(ks[10], (DEPTH, CONV_WIDTH)),
        "conv_ln_g": gain(ks[11], (DEPTH, CONV_WIDTH)),
        "conv_ln_b": bias(ks[12], (DEPTH, CONV_WIDTH)),
        "conv_w_o": nrm(ks[13], (DEPTH, CONV_WIDTH, D_MODEL), beta * CONV_WIDTH ** -0.5),
        "gla_gate_w2": nrm(ks[14], (DEPTH, 2, GLA_GATE_RANK, GLA_KEY_WIDTH), GLA_GATE_RANK ** -0.5),
        "gla_gate_b": bias(ks[15], (DEPTH, 2, GLA_KEY_WIDTH)),
        "gla_norm_g": gain(ks[16], (DEPTH, GLA_DV)),
        "gla_w_o": nrm(ks[17], (DEPTH, GLA_VAL_WIDTH, D_MODEL), beta * GLA_VAL_WIDTH ** -0.5),
        "w_out": nrm(ks[18], (DEPTH, D_MODEL, D_MODEL), beta * D_MODEL ** -0.5),
        "ln1_g": gain(ks[19], (DEPTH, D_MODEL)),
        "ln1_b": bias(ks[20], (DEPTH, D_MODEL)),
        "w_router": nrm(ks[21], (DEPTH, D_MODEL, N_EXPERTS), D_MODEL ** -0.5),
        "w_gate": nrm(ks[22], (DEPTH, N_EXPERTS, D_MODEL, EXPERT_FF), D_MODEL ** -0.5),
        "w_up": nrm(ks[23], (DEPTH, N_EXPERTS, D_MODEL, EXPERT_FF), D_MODEL ** -0.5),
        "w_down": nrm(ks[24], (DEPTH, N_EXPERTS, EXPERT_FF, D_MODEL), beta * EXPERT_FF ** -0.5),
        "ln2_g": gain(ks[25], (DEPTH, D_MODEL)),
        "ln2_b": bias(ks[26], (DEPTH, D_MODEL)),
    }


def reference(x_prompt, x_sample, w_in, da_lam_q1, da_lam_k1, da_lam_q2, da_lam_k2, da_subln_g, da_w_o,
              conv_w, conv_b, conv_ln_g, conv_ln_b, conv_w_o,
              gla_gate_w2, gla_gate_b, gla_norm_g, gla_w_o,
              w_out, ln1_g, ln1_b, w_router, w_gate, w_up, w_down, ln2_g, ln2_b):
    y_prompt = encode(x_prompt, w_in, da_lam_q1, da_lam_k1, da_lam_q2, da_lam_k2, da_subln_g, da_w_o,
                      conv_w, conv_b, conv_ln_g, conv_ln_b, conv_w_o,
                      gla_gate_w2, gla_gate_b, gla_norm_g, gla_w_o,
                      w_out, ln1_g, ln1_b, w_router, w_gate, w_up, w_down, ln2_g, ln2_b)
    y_sample = encode(x_sample, w_in, da_lam_q1, da_lam_k1, da_lam_q2, da_lam_k2, da_subln_g, da_w_o,
                      conv_w, conv_b, conv_ln_g, conv_ln_b, conv_w_o,
                      gla_gate_w2, gla_gate_b, gla_norm_g, gla_w_o,
                      w_out, ln1_g, ln1_b, w_router, w_gate, w_up, w_down, ln2_g, ln2_b)
    return (y_prompt, y_sample)
```

```python
import functools
import math

import jax
import jax.numpy as jnp
from jax import lax
from jax.experimental import pallas as pl
from jax.experimental.pallas import tpu as pltpu

F32 = jnp.float32
BF16 = jnp.bfloat16
I32 = jnp.int32

D_MODEL = 1024
DEPTH = 4
DA_HEADS = 4
DA_HEAD_DIM = 64
DA_WIDTH = DA_HEADS * 2 * DA_HEAD_DIM
ROPE_THETA = 10000.0
CONV_WIDTH = 512
CONV_KERNEL = 31
GLA_HEADS = 4
GLA_DK = 64
GLA_DV = 128
GLA_KEY_WIDTH = GLA_HEADS * GLA_DK
GLA_VAL_WIDTH = GLA_HEADS * GLA_DV
GLA_GATE_RANK = 16
GLA_TAU = 16.0
GLA_CHUNK = 64
N_BRANCH = 3
N_EXPERTS = 16
EXPERT_FF = 2048
CAPACITY_FACTOR = 2
DEEPNORM_ALPHA = (2 * DEPTH) ** 0.25
NORM_EPS = 1e-5

LANES = 128
TOKEN_CHUNK = 256
SLOT_WINDOW = 256
SLOT_ALIGN = 16
CONV_HALO = 16
VMEM_LIMIT = 56 * 1024 * 1024

_NT = (((1,), (1,)), ((), ()))
_TN = (((0,), (0,)), ((), ()))


def _cparams(*sem):
    return pltpu.CompilerParams(dimension_semantics=sem, vmem_limit_bytes=VMEM_LIMIT)


def _dot(a, b):
    return jnp.dot(a, b, preferred_element_type=F32)


def _sigmoid(x):
    return 1.0 / (1.0 + jnp.exp(-x))


def _layernorm(y, g, b):
    mu = jnp.mean(y, axis=-1, keepdims=True)
    yc = y - mu
    var = jnp.mean(yc * yc, axis=-1, keepdims=True)
    return yc * lax.rsqrt(var + NORM_EPS) * g + b


def _proj_kernel(x_ref, cos_ref, sin_ref, wqk_ref, wv_ref, wcu_ref, wg4_ref, wlr_ref, gw_ref, gb_ref,
                 q_ref, k_ref, v_ref, u_ref, gq_ref, gk_ref, gv_ref, gr_ref, gf_ref, gbw_ref):
    xb = x_ref[...].astype(BF16)
    tm = xb.shape[0]
    qk = _dot(xb, wqk_ref[...])
    cos = cos_ref[...]
    sin = sin_ref[...]
    lane = lax.broadcasted_iota(I32, (tm, LANES), 1)
    first_half = (lane % DA_HEAD_DIM) < (DA_HEAD_DIM // 2)
    nqb = DA_WIDTH // LANES
    for cb in range(2 * nqb):
        xc = qk[:, cb * LANES:(cb + 1) * LANES]
        rot = jnp.where(first_half, pltpu.roll(xc, LANES - DA_HEAD_DIM // 2, 1), pltpu.roll(xc, DA_HEAD_DIM // 2, 1))
        r = xc * cos + rot * sin
        if cb < nqb:
            q_ref[:, cb * LANES:(cb + 1) * LANES] = (r * (DA_HEAD_DIM ** -0.5)).astype(BF16)
        else:
            k_ref[:, (cb - nqb) * LANES:(cb - nqb + 1) * LANES] = r.astype(BF16)
    v_ref[...] = _dot(xb, wv_ref[...]).astype(BF16)
    cu = _dot(xb, wcu_ref[...])
    u_ref[...] = cu[:, :CONV_WIDTH] * _sigmoid(cu[:, CONV_WIDTH:])
    g4 = _dot(xb, wg4_ref[...])
    kw, vw = GLA_KEY_WIDTH, GLA_VAL_WIDTH
    gq_ref[...] = g4[:, :kw] * (GLA_DK ** -0.5)
    gk_ref[...] = g4[:, kw:2 * kw]
    gv_ref[...] = g4[:, 2 * kw:2 * kw + vw]
    gr_ref[...] = g4[:, 2 * kw + vw:]
    lr = _dot(xb, wlr_ref[...])
    gp = _dot(lr.astype(BF16), gw_ref[...]) + gb_ref[...]
    lsig = jnp.minimum(gp, 0.0) - jnp.log(1.0 + jnp.exp(-jnp.abs(gp)))
    lsig = lsig * (1.0 / GLA_TAU)
    gf_ref[...] = lsig[:, :kw]
    gbw_ref[...] = lsig[:, kw:]


def _proj(x, cos, sin, w, seq, tm):
    nt = x.shape[0]
    nseq = seq // tm
    row = lambda i: (i, 0)
    const = lambda i: (0, 0)
    full = lambda a: pl.BlockSpec(a.shape, const)
    outs = [(DA_WIDTH, BF16), (DA_WIDTH, BF16), (DA_WIDTH, BF16), (CONV_WIDTH, F32),
            (GLA_KEY_WIDTH, F32), (GLA_KEY_WIDTH, F32), (GLA_VAL_WIDTH, F32), (GLA_VAL_WIDTH, F32),
            (GLA_KEY_WIDTH, F32), (GLA_KEY_WIDTH, F32)]
    return pl.pallas_call(
        _proj_kernel,
        grid=(nt // tm,),
        in_specs=[pl.BlockSpec((tm, D_MODEL), row),
                  pl.BlockSpec((tm, LANES), lambda i: (i % nseq, 0)),
                  pl.BlockSpec((tm, LANES), lambda i: (i % nseq, 0)),
                  full(w["wqk"]), full(w["wv"]), full(w["wcu"]), full(w["wg4"]), full(w["wlr"]),
                  full(w["gw"]), full(w["gb"])],
        out_specs=[pl.BlockSpec((tm, c), row) for c, _ in outs],
        out_shape=[jax.ShapeDtypeStruct((nt, c), d) for c, d in outs],
        compiler_params=_cparams("parallel"),
        name="proj",
    )(x, cos, sin, w["wqk"], w["wv"], w["wcu"], w["wg4"], w["wlr"], w["gw"], w["gb"])


def _attn_kernel(lamp_ref, g_ref, q_ref, k_ref, v_ref, o_ref, *, lam_init):
    lp = lamp_ref[...]
    lam = (jnp.exp(jnp.sum(lp[0:1] * lp[1:2], axis=1, keepdims=True))
           - jnp.exp(jnp.sum(lp[2:3] * lp[3:4], axis=1, keepdims=True)) + lam_init)
    q = q_ref[...]
    k = k_ref[...]
    lane = lax.broadcasted_iota(I32, q.shape, 1)
    zero = jnp.zeros_like(q)
    s0 = lax.dot_general(jnp.where(lane < DA_HEAD_DIM, q, zero), k, _NT, preferred_element_type=F32)
    s1 = lax.dot_general(jnp.where(lane >= DA_HEAD_DIM, q, zero), k, _NT, preferred_element_type=F32)
    e0 = jnp.exp(s0 - jnp.max(s0, axis=1, keepdims=True))
    e1 = jnp.exp(s1 - jnp.max(s1, axis=1, keepdims=True))
    r0 = 1.0 / jnp.sum(e0, axis=1, keepdims=True)
    r1 = lam / jnp.sum(e1, axis=1, keepdims=True)
    p = e0 * r0 - e1 * r1
    o = _dot(p.astype(BF16), v_ref[...])
    ms = jnp.mean(o * o, axis=1, keepdims=True)
    y = o * lax.rsqrt(ms + NORM_EPS) * g_ref[...] * (1.0 - lam_init)
    o_ref[...] = y.astype(BF16)


def _attn(q, k, v, lamp, subln_g, lam_init, nbatch, seq, tq):
    nt = q.shape[0]
    nq = seq // tq
    return pl.pallas_call(
        functools.partial(_attn_kernel, lam_init=lam_init),
        grid=(nbatch, DA_HEADS, nq),
        in_specs=[pl.BlockSpec(lamp.shape, lambda b, h, i: (0, 0)),
                  pl.BlockSpec(subln_g.shape, lambda b, h, i: (0, 0)),
                  pl.BlockSpec((tq, LANES), lambda b, h, i: (b * nq + i, h)),
                  pl.BlockSpec((seq, LANES), lambda b, h, i: (b, h)),
                  pl.BlockSpec((seq, LANES), lambda b, h, i: (b, h))],
        out_specs=pl.BlockSpec((tq, LANES), lambda b, h, i: (b * nq + i, h)),
        out_shape=jax.ShapeDtypeStruct((nt, DA_WIDTH), BF16),
        compiler_params=_cparams("parallel", "parallel", "parallel"),
        name="diff_attn",
    )(lamp, subln_g, q, k, v)


CONV_ROWS = 32


def _conv_kernel(prev_ref, cur_ref, next_ref, w_ref, cb_ref, lg_ref, lb_ref, o_ref, scr, *, ts, nblk):
    i = pl.program_id(1)
    scr[0:CONV_HALO, :] = jnp.where(i > 0, prev_ref[...], 0.0)
    scr[CONV_HALO:CONV_HALO + ts, :] = cur_ref[...]
    scr[CONV_HALO + ts:2 * CONV_HALO + ts, :] = jnp.where(i < nblk - 1, next_ref[...], 0.0)
    w = w_ref[...]
    base = CONV_HALO - CONV_KERNEL // 2
    for rb in range(ts // CONV_ROWS):
        r0 = rb * CONV_ROWS
        acc = jnp.zeros((CONV_ROWS, CONV_WIDTH), F32)
        for t in range(CONV_KERNEL):
            acc = acc + w[t:t + 1, :] * scr[r0 + base + t:r0 + base + t + CONV_ROWS, :]
        h = _layernorm(acc + cb_ref[...], lg_ref[...], lb_ref[...])
        o_ref[r0:r0 + CONV_ROWS, :] = (h * _sigmoid(h)).astype(BF16)


def _conv(u, w, cb, lg, lb, nbatch, seq, ts):
    nt = u.shape[0]
    nblk = seq // ts
    hb = ts // CONV_HALO
    nhalo = nt // CONV_HALO
    const = lambda b, i: (0, 0)
    return pl.pallas_call(
        functools.partial(_conv_kernel, ts=ts, nblk=nblk),
        grid=(nbatch, nblk),
        in_specs=[pl.BlockSpec((CONV_HALO, CONV_WIDTH), lambda b, i: (jnp.maximum((b * nblk + i) * hb - 1, 0), 0)),
                  pl.BlockSpec((ts, CONV_WIDTH), lambda b, i: (b * nblk + i, 0)),
                  pl.BlockSpec((CONV_HALO, CONV_WIDTH),
                               lambda b, i: (jnp.minimum((b * nblk + i + 1) * hb, nhalo - 1), 0)),
                  pl.BlockSpec(w.shape, const), pl.BlockSpec(cb.shape, const),
                  pl.BlockSpec(lg.shape, const), pl.BlockSpec(lb.shape, const)],
        out_specs=pl.BlockSpec((ts, CONV_WIDTH), lambda b, i: (b * nblk + i, 0)),
        out_shape=jax.ShapeDtypeStruct((nt, CONV_WIDTH), BF16),
        scratch_shapes=[pltpu.VMEM((ts + 2 * CONV_HALO, CONV_WIDTH), F32)],
        compiler_params=_cparams("parallel", "parallel"),
        name="conformer_conv",
    )(u, u, u, w, cb, lg, lb)


def _split3(g):
    hi = g.astype(BF16)
    r1 = g - hi.astype(F32)
    mid = r1.astype(BF16)
    lo = (r1 - mid.astype(F32)).astype(BF16)
    return hi, mid, lo


def _gla_dir(q_ref, k_ref, v_ref, g_ref, o_ref, st_ref, d, nchunk, reverse):
    c = GLA_CHUNK
    r = lax.broadcasted_iota(I32, (c, c), 0)
    cc = lax.broadcasted_iota(I32, (c, c), 1)
    keep = (cc >= r) if reverse else (cc <= r)
    tri = jnp.where(keep, 1.0, 0.0).astype(BF16)
    lane = lax.broadcasted_iota(I32, (c, LANES), 1)
    order = range(nchunk - 1, -1, -1) if reverse else range(nchunk)
    for ci in order:
        rows = slice(ci * c, (ci + 1) * c)
        hi, mid, lo = _split3(g_ref[rows, :])
        b = _dot(tri, hi) + _dot(tri, mid) + _dot(tri, lo)
        b_last = b[0:1, :] if reverse else b[c - 1:c, :]
        q = q_ref[rows, :]
        k = k_ref[rows, :]
        qt = (q * jnp.exp(b)).astype(BF16)
        kt = (k * jnp.exp(-b)).astype(BF16)
        kd = (k * jnp.exp(b_last - b)).astype(BF16)
        decay = jnp.exp(b_last)
        for h in range(2):
            in_head = (lane >= h * GLA_DK) & (lane < (h + 1) * GLA_DK)
            qm = jnp.where(in_head, qt, jnp.zeros_like(qt))
            a = lax.dot_general(qm, kt, _NT, preferred_element_type=F32)
            a = jnp.where(keep, a, 0.0).astype(BF16)
            vh = v_ref[rows, h * GLA_DV:(h + 1) * GLA_DV].astype(BF16)
            st = st_ref[d, h]
            o = _dot(a, vh) + lax.dot_general(qm, st.astype(BF16), _NT, preferred_element_type=F32)
            o_ref[rows, h * GLA_DV:(h + 1) * GLA_DV] = o
            kv = lax.dot_general(vh, kd, _TN, preferred_element_type=F32)
            st_ref[d, h] = st * decay + kv


def _gla_kernel(qf_ref, kf_ref, vf_ref, gf_ref, qb_ref, kb_ref, vb_ref, gb_ref, of_ref, ob_ref, st_ref, *, tc):
    @pl.when(pl.program_id(2) == 0)
    def _():
        st_ref[...] = jnp.zeros_like(st_ref)

    nchunk = tc // GLA_CHUNK
    _gla_dir(qf_ref, kf_ref, vf_ref, gf_ref, of_ref, st_ref, 0, nchunk, False)
    _gla_dir(qb_ref, kb_ref, vb_ref, gb_ref, ob_ref, st_ref, 1, nchunk, True)


def _gla(gq, gk, gv, gf, gb, nbatch, seq, tc):
    nt = gq.shape[0]
    nj = seq // tc
    fwd = lambda b, p, j: (b * nj + j, p)
    bwd = lambda b, p, j: (b * nj + nj - 1 - j, p)
    kspec = lambda m: pl.BlockSpec((tc, LANES), m)
    vspec = lambda m: pl.BlockSpec((tc, 2 * GLA_DV), m)
    return pl.pallas_call(
        functools.partial(_gla_kernel, tc=tc),
        grid=(nbatch, GLA_HEADS // 2, nj),
        in_specs=[kspec(fwd), kspec(fwd), vspec(fwd), kspec(fwd), kspec(bwd), kspec(bwd), vspec(bwd), kspec(bwd)],
        out_specs=[vspec(fwd), vspec(bwd)],
        out_shape=[jax.ShapeDtypeStruct((nt, GLA_VAL_WIDTH), F32)] * 2,
        scratch_shapes=[pltpu.VMEM((2, 2, GLA_DV, LANES), F32)],
        compiler_params=_cparams("parallel", "parallel", "arbitrary"),
        name="bi_gla",
    )(gq, gk, gv, gf, gq, gk, gv, gb)


def _merge_kernel(x_ref, ao_ref, ch_ref, of_ref, ob_ref, gr_ref, ng_ref, wa_ref, wc_ref, wl_ref, wmg_ref, wout_ref,
                  g1_ref, b1_ref, wr_ref, x1_ref, x1b_ref, aff_ref):
    x = x_ref[...]
    xb = x.astype(BF16)
    a = _dot(ao_ref[...], wa_ref[...])
    c = _dot(ch_ref[...], wc_ref[...])
    o = of_ref[...] + ob_ref[...]
    r = gr_ref[...]
    ng = ng_ref[...]
    parts = []
    for h in range(GLA_HEADS):
        oh = o[:, h * GLA_DV:(h + 1) * GLA_DV]
        rh = r[:, h * GLA_DV:(h + 1) * GLA_DV]
        yh = oh * lax.rsqrt(jnp.mean(oh * oh, axis=1, keepdims=True) + NORM_EPS) * ng
        parts.append((yh * (rh * _sigmoid(rh))).astype(BF16))
    l = _dot(jnp.concatenate(parts, axis=1), wl_ref[...])
    m = _sigmoid(_dot(xb, wmg_ref[:, 0:D_MODEL])) * a
    m = m + _sigmoid(_dot(xb, wmg_ref[:, D_MODEL:2 * D_MODEL])) * c
    m = m + _sigmoid(_dot(xb, wmg_ref[:, 2 * D_MODEL:3 * D_MODEL])) * l
    y = DEEPNORM_ALPHA * x + _dot(m.astype(BF16), wout_ref[...])
    x1 = _layernorm(y, g1_ref[...], b1_ref[...])
    x1_ref[...] = x1
    x1b = x1.astype(BF16)
    x1b_ref[...] = x1b
    logits = lax.dot_general(wr_ref[...], x1b, _NT, preferred_element_type=F32)
    ex = jnp.exp(logits - jnp.max(logits, axis=0, keepdims=True))
    aff_ref[...] = ex / jnp.sum(ex, axis=0, keepdims=True)


def _merge(x, ao, ch, of, ob, gr, w, tm):
    nt = x.shape[0]
    row = lambda i: (i, 0)
    const = lambda i: (0, 0)
    full = lambda a: pl.BlockSpec(a.shape, const)
    names = ["ng", "wa", "wc", "wl", "wmg", "wout", "g1", "b1", "wr"]
    return pl.pallas_call(
        _merge_kernel,
        grid=(nt // tm,),
        in_specs=[pl.BlockSpec((tm, D_MODEL), row), pl.BlockSpec((tm, DA_WIDTH), row),
                  pl.BlockSpec((tm, CONV_WIDTH), row), pl.BlockSpec((tm, GLA_VAL_WIDTH), row),
                  pl.BlockSpec((tm, GLA_VAL_WIDTH), row), pl.BlockSpec((tm, GLA_VAL_WIDTH), row)]
                 + [full(w[n]) for n in names],
        out_specs=[pl.BlockSpec((tm, D_MODEL), row), pl.BlockSpec((tm, D_MODEL), row),
                   pl.BlockSpec((N_EXPERTS, tm), lambda i: (0, i))],
        out_shape=[jax.ShapeDtypeStruct((nt, D_MODEL), F32), jax.ShapeDtypeStruct((nt, D_MODEL), BF16),
                   jax.ShapeDtypeStruct((N_EXPERTS, nt), F32)],
        compiler_params=_cparams("parallel"),
        name="merge_ln1_router",
    )(x, ao, ch, of, ob, gr, *[w[n] for n in names])


def _select_kernel(aff_ref, pos_ref, offs_ref, sel_ref, *, cap, slot_base, idx_bits):
    ne, n = aff_ref.shape
    bits = pltpu.bitcast(aff_ref[...], I32)
    capf = float(cap)

    def count(mask):
        return jnp.sum(jnp.where(mask, 1.0, 0.0), axis=1, keepdims=True)

    def value_step(i, t):
        cand = t | jnp.left_shift(jnp.int32(1), 30 - i)
        return jnp.where(count(bits >= cand) >= capf, cand, t)

    thr = lax.fori_loop(0, 31, value_step, jnp.zeros((ne, 1), I32))
    above = bits > thr
    tie = bits == thr
    need = capf - count(above)
    idx = lax.broadcasted_iota(I32, (ne, n), 1)

    def index_step(i, ans):
        cand = ans | jnp.left_shift(jnp.int32(1), idx_bits - 1 - i)
        return jnp.where(count(tie & (idx < cand)) < need, cand, ans)

    last_tie = lax.fori_loop(0, idx_bits, index_step, jnp.zeros((ne, 1), I32))
    sel_ref[...] = jnp.where(above | (tie & (idx <= last_tie)), 1.0, 0.0)

    ch = TOKEN_CHUNK
    upper = jnp.where(lax.broadcasted_iota(I32, (ch, ch), 0) <= lax.broadcasted_iota(I32, (ch, ch), 1),
                      1.0, 0.0).astype(BF16)
    lane = lax.broadcasted_iota(I32, (ne, LANES), 1)

    offs_ref[...] = jnp.zeros_like(offs_ref)

    def chunk_step(j, run):
        off = pl.multiple_of(j * ch, ch)
        m = sel_ref[:, pl.ds(off, ch)]
        incl = _dot(m.astype(BF16), upper)
        p = jnp.where(m > 0.0, run + incl + (slot_base - 1.0), -1.0)
        pos_ref[:, pl.ds(off, ch)] = p.astype(I32)
        offs_ref[...] = jnp.where(lane == j, jnp.broadcast_to(run, (ne, LANES)).astype(I32), offs_ref[...])
        return run + incl[:, ch - 1:ch]

    lax.fori_loop(0, n // ch, chunk_step, jnp.zeros((ne, 1), F32))


def _select(aff, cap, slot_base):
    ne, n = aff.shape
    assert n % TOKEN_CHUNK == 0 and n // TOKEN_CHUNK <= LANES
    return pl.pallas_call(
        functools.partial(_select_kernel, cap=cap, slot_base=slot_base, idx_bits=max(1, (n - 1).bit_length())),
        out_shape=[jax.ShapeDtypeStruct((ne, n), I32), jax.ShapeDtypeStruct((ne, LANES), I32)],
        scratch_shapes=[pltpu.VMEM((ne, n), F32)],
        compiler_params=pltpu.CompilerParams(vmem_limit_bytes=VMEM_LIMIT),
        name="expert_select",
    )(aff)


def _ffn_kernel(blo_ref, bhi_ref, pos_ref, x_hbm, wg_ref, wu_ref, wd_ref, ye_ref, xbuf, sem, acc_ref, *, ts, ntile):
    e = pl.program_id(0)
    j = pl.program_id(1)
    lo = blo_ref[e * ntile + j]
    n = bhi_ref[e * ntile + j] - lo
    ch = TOKEN_CHUNK

    def copy(c, slot):
        return pltpu.make_async_copy(x_hbm.at[pl.ds(pl.multiple_of(c * ch, ch), ch), :], xbuf.at[slot], sem.at[slot])

    copy(lo, 0).start()
    acc_ref[...] = jnp.zeros_like(acc_ref)
    slot_id = lax.broadcasted_iota(I32, (ts, ch), 0) + j * ts

    def chunk_step(c, carry):
        slot = c % 2
        copy(lo + c, slot).wait()

        @pl.when(c + 1 < n)
        def _():
            copy(lo + c + 1, 1 - slot).start()

        prow = pos_ref[:, pl.ds(pl.multiple_of((lo + c) * ch, ch), ch)]
        onehot = jnp.where(prow == slot_id, 1.0, 0.0).astype(BF16)
        acc_ref[...] += _dot(onehot, xbuf[slot])
        return carry

    lax.fori_loop(0, n, chunk_step, 0)
    xe = acc_ref[...].astype(BF16)
    g = _dot(xe, wg_ref[...])
    u = _dot(xe, wu_ref[...])
    h = (g * _sigmoid(g)) * u
    ye_ref[...] = _dot(h.astype(BF16), wd_ref[...]).astype(BF16)


def _ffn(blo, bhi, pos3, x1b, wg, wu, wd, cap_total, ts):
    ntile = cap_total // ts
    nt = x1b.shape[0]
    return pl.pallas_call(
        functools.partial(_ffn_kernel, ts=ts, ntile=ntile),
        grid_spec=pltpu.PrefetchScalarGridSpec(
            num_scalar_prefetch=2,
            grid=(N_EXPERTS, ntile),
            in_specs=[pl.BlockSpec((None, 1, nt), lambda e, j, *_: (e, 0, 0)),
                      pl.BlockSpec(memory_space=pl.ANY),
                      pl.BlockSpec((None, D_MODEL, EXPERT_FF), lambda e, j, *_: (e, 0, 0)),
                      pl.BlockSpec((None, D_MODEL, EXPERT_FF), lambda e, j, *_: (e, 0, 0)),
                      pl.BlockSpec((None, EXPERT_FF, D_MODEL), lambda e, j, *_: (e, 0, 0))],
            out_specs=pl.BlockSpec((None, ts, D_MODEL), lambda e, j, *_: (e, j, 0)),
            scratch_shapes=[pltpu.VMEM((2, TOKEN_CHUNK, D_MODEL), BF16), pltpu.SemaphoreType.DMA((2,)),
                            pltpu.VMEM((ts, D_MODEL), F32)]),
        out_shape=jax.ShapeDtypeStruct((N_EXPERTS, cap_total, D_MODEL), BF16),
        compiler_params=_cparams("arbitrary", "arbitrary"),
        name="gather_expert_ffn",
    )(blo, bhi, pos3, x1b, wg, wu, wd)


def _combine_kernel(st_ref, nw_ref, x1_ref, post_ref, affn_ref, g2_ref, b2_ref, ye_hbm, out_ref, buf, sem, acc_ref,
                    *, cap_total):
    i = pl.program_id(0)
    tm = x1_ref.shape[0]
    w = SLOT_WINDOW

    def window(e, k):
        lo = st_ref[i * N_EXPERTS + e] + k * w
        return lo, pl.multiple_of(jnp.minimum(lo, cap_total - w), SLOT_ALIGN)

    def copy(e, k, slot):
        return pltpu.make_async_copy(ye_hbm.at[e, pl.ds(window(e, k)[1], w), :], buf.at[slot], sem.at[slot])

    copy(0, 0, 0).start()
    acc_ref[...] = jnp.zeros_like(acc_ref)
    lane = lax.broadcasted_iota(I32, (tm, w), 1)
    done = jnp.int32(0)
    for e in range(N_EXPERTS):
        nw = nw_ref[i * N_EXPERTS + e]
        pe = post_ref[:, e:e + 1]
        ge = affn_ref[:, e:e + 1]

        def window_step(k, done, e=e, nw=nw, pe=pe, ge=ge):
            slot = done % 2
            copy(e, k, slot).wait()
            last = k + 1 >= nw

            @pl.when(jnp.logical_not(last))
            def _():
                copy(e, k + 1, 1 - slot).start()

            if e + 1 < N_EXPERTS:
                @pl.when(last)
                def _():
                    copy(e + 1, 0, 1 - slot).start()

            lo, start = window(e, k)
            onehot = jnp.where((pe == start + lane) & (pe >= lo), 1.0, 0.0).astype(BF16)
            acc_ref[...] += ge * _dot(onehot, buf[slot])
            return done + 1

        done = lax.fori_loop(0, nw, window_step, done)
    y = DEEPNORM_ALPHA * x1_ref[...] + acc_ref[...]
    out_ref[...] = _layernorm(y, g2_ref[...], b2_ref[...])


def _combine(starts, nwin, x1, post, affn, g2, b2, ye, tm):
    nt = x1.shape[0]
    cap_total = ye.shape[1]
    row = lambda i, *_: (i, 0)
    const = lambda i, *_: (0, 0)
    return pl.pallas_call(
        functools.partial(_combine_kernel, cap_total=cap_total),
        grid_spec=pltpu.PrefetchScalarGridSpec(
            num_scalar_prefetch=2,
            grid=(nt // tm,),
            in_specs=[pl.BlockSpec((tm, D_MODEL), row), pl.BlockSpec((tm, N_EXPERTS), row),
                      pl.BlockSpec((tm, N_EXPERTS), row), pl.BlockSpec(g2.shape, const),
                      pl.BlockSpec(b2.shape, const), pl.BlockSpec(memory_space=pl.ANY)],
            out_specs=pl.BlockSpec((tm, D_MODEL), row),
            scratch_shapes=[pltpu.VMEM((2, SLOT_WINDOW, D_MODEL), BF16), pltpu.SemaphoreType.DMA((2,)),
                            pltpu.VMEM((tm, D_MODEL), F32)]),
        out_shape=jax.ShapeDtypeStruct((nt, D_MODEL), F32),
        compiler_params=_cparams("arbitrary"),
        name="combine_ln2",
    )(starts, nwin, x1, post, affn, g2, b2, ye)


def _tile_bounds(offs_groups, caps, chunk_bases, ts):
    los, his = [], []
    for offs, cap, cbase in zip(offs_groups, caps, chunk_bases):
        nch = offs.shape[1]
        ends = jnp.concatenate([offs[:, 1:], jnp.full((N_EXPERTS, 1), cap, I32)], axis=1)
        s0 = (jnp.arange(cap // ts, dtype=I32) * ts)[None, :, None]
        los.append(cbase + jnp.sum((ends[:, None, :] <= s0).astype(I32), axis=2))
        his.append(cbase + jnp.sum((offs[:, None, :] < s0 + ts).astype(I32), axis=2))
        del nch
    return jnp.concatenate(los, axis=1).reshape(-1), jnp.concatenate(his, axis=1).reshape(-1)


def _window_bounds(offs_groups, caps, slot_bases, tm):
    starts, nwins = [], []
    per = tm // TOKEN_CHUNK
    for offs, cap, sbase in zip(offs_groups, caps, slot_bases):
        ends = jnp.concatenate([offs, jnp.full((N_EXPERTS, 1), cap, I32)], axis=1)
        first = ends[:, 0:-1:per] + sbase
        stop = ends[:, per::per] + sbase
        st = (first // SLOT_ALIGN) * SLOT_ALIGN
        nw = jnp.maximum((stop - st + SLOT_WINDOW - 1) // SLOT_WINDOW, 1)
        starts.append(st.T)
        nwins.append(nw.T)
    return jnp.concatenate(starts, axis=0).reshape(-1), jnp.concatenate(nwins, axis=0).reshape(-1)


def _prep_layer(l, p):
    bf = lambda a: a.astype(BF16)
    w_in = p["w_in"][l]
    c = 0
    cuts = {}
    for name, width in (("q", DA_WIDTH), ("k", DA_WIDTH), ("v", DA_WIDTH), ("cu", 2 * CONV_WIDTH),
                        ("g4", 2 * GLA_KEY_WIDTH + 2 * GLA_VAL_WIDTH), ("lr", 2 * GLA_GATE_RANK),
                        ("mg", N_BRANCH * D_MODEL)):
        cuts[name] = w_in[:, c:c + width]
        c += width
    gw2 = p["gla_gate_w2"][l]
    gw = jnp.zeros((LANES, 2 * GLA_KEY_WIDTH), F32)
    gw = gw.at[0:GLA_GATE_RANK, 0:GLA_KEY_WIDTH].set(gw2[0])
    gw = gw.at[GLA_GATE_RANK:2 * GLA_GATE_RANK, GLA_KEY_WIDTH:].set(gw2[1])
    lamp = jnp.zeros((8, LANES), F32)
    for r, nm in enumerate(("da_lam_q1", "da_lam_k1", "da_lam_q2", "da_lam_k2")):
        lamp = lamp.at[r, 0:DA_HEAD_DIM].set(p[nm][l])
    row = lambda a: a.reshape(1, -1)
    return {
        "wqk": bf(jnp.concatenate([cuts["q"], cuts["k"]], axis=1)), "wv": bf(cuts["v"]), "wcu": bf(cuts["cu"]),
        "wg4": bf(cuts["g4"]), "wlr": bf(jnp.pad(cuts["lr"], ((0, 0), (0, LANES - 2 * GLA_GATE_RANK)))),
        "gw": bf(gw), "gb": p["gla_gate_b"][l].reshape(1, -1),
        "lamp": lamp, "subln": row(p["da_subln_g"][l]),
        "conv_w": jnp.pad(p["conv_w"][l], ((0, 1), (0, 0))), "conv_b": row(p["conv_b"][l]),
        "conv_lg": row(p["conv_ln_g"][l]), "conv_lb": row(p["conv_ln_b"][l]),
        "ng": row(p["gla_norm_g"][l]), "wa": bf(p["da_w_o"][l]), "wc": bf(p["conv_w_o"][l]),
        "wl": bf(p["gla_w_o"][l]), "wmg": bf(cuts["mg"]), "wout": bf(p["w_out"][l]),
        "g1": row(p["ln1_g"][l]), "b1": row(p["ln1_b"][l]), "wr": bf(p["w_router"][l].T),
        "wgate": bf(p["w_gate"][l]), "wup": bf(p["w_up"][l]), "wdown": bf(p["w_down"][l]),
        "g2": row(p["ln2_g"][l]), "b2": row(p["ln2_b"][l]),
    }


def _rope_tables(seq):
    d = DA_HEAD_DIM
    inv = 1.0 / (ROPE_THETA ** (jnp.arange(0, d, 2, dtype=F32) / d))
    ang = jnp.arange(seq, dtype=F32)[:, None] * inv[None, :]
    c, s = jnp.cos(ang), jnp.sin(ang)
    cos = jnp.concatenate([c, c], axis=1)
    sin = jnp.concatenate([-s, s], axis=1)
    reps = LANES // d
    return jnp.tile(cos, (1, reps)), jnp.tile(sin, (1, reps))


def _tiles(seq, group_tokens, caps):
    g = functools.reduce(math.gcd, group_tokens)
    return {
        "proj": min(512, seq), "attn": min(256, seq), "conv": min(256, seq), "gla": min(512, seq),
        "merge": min(512, seq), "ffn": min(512, functools.reduce(math.gcd, caps)), "combine": min(1024, g),
    }


def _encode(xs, p, depth):
    seq = xs[0].shape[1]
    nbatch = sum(x.shape[0] for x in xs)
    group_tokens = [x.shape[0] * seq for x in xs]
    caps = [CAPACITY_FACTOR * n // N_EXPERTS for n in group_tokens]
    slot_bases = [sum(caps[:i]) for i in range(len(caps))]
    token_bases = [sum(group_tokens[:i]) for i in range(len(caps))]
    chunk_bases = [t // TOKEN_CHUNK for t in token_bases]
    cap_total = sum(caps)
    t = _tiles(seq, group_tokens, caps)
    assert cap_total >= SLOT_WINDOW and all(c % t["ffn"] == 0 for c in caps)
    x = jnp.concatenate([x.reshape(-1, D_MODEL) for x in xs], axis=0)
    cos, sin = _rope_tables(seq)
    for l in range(depth):
        w = _prep_layer(l, p)
        lam_init = 0.8 - 0.6 * math.exp(-0.3 * l)
        q, k, v, u, gq, gk, gv, gr, gf, gb = _proj(x, cos, sin, w, seq, t["proj"])
        ao = _attn(q, k, v, w["lamp"], w["subln"], lam_init, nbatch, seq, t["attn"])
        ch = _conv(u, w["conv_w"], w["conv_b"], w["conv_lg"], w["conv_lb"], nbatch, seq, t["conv"])
        of, ob = _gla(gq, gk, gv, gf, gb, nbatch, seq, t["gla"])
        x1, x1b, aff = _merge(x, ao, ch, of, ob, gr, w, t["merge"])
        pos_g, offs_g = [], []
        for n, tb, cap, sb in zip(group_tokens, token_bases, caps, slot_bases):
            pos, offs = _select(aff[:, tb:tb + n], cap, sb)
            pos_g.append(pos)
            offs_g.append(offs[:, :n // TOKEN_CHUNK])
        pos = jnp.concatenate(pos_g, axis=1)
        blo, bhi = _tile_bounds(offs_g, caps, chunk_bases, t["ffn"])
        ye = _ffn(blo, bhi, pos.reshape(N_EXPERTS, 1, -1), x1b, w["wgate"], w["wup"], w["wdown"], cap_total, t["ffn"])
        starts, nwin = _window_bounds(offs_g, caps, slot_bases, t["combine"])
        x = _combine(starts, nwin, x1, pos.T, aff.T, w["g2"], w["b2"], ye, t["combine"])
    outs, o = [], 0
    for xg, n in zip(xs, group_tokens):
        outs.append(x[o:o + n].reshape(xg.shape))
        o += n
    return tuple(outs)


def kernel(x_prompt, x_sample, w_in, da_lam_q1, da_lam_k1, da_lam_q2, da_lam_k2, da_subln_g, da_w_o, conv_w, conv_b, conv_ln_g, conv_ln_b, conv_w_o, gla_gate_w2, gla_gate_b, gla_norm_g, gla_w_o, w_out, ln1_g, ln1_b, w_router, w_gate, w_up, w_down, ln2_g, ln2_b):
    p = dict(w_in=w_in, da_lam_q1=da_lam_q1, da_lam_k1=da_lam_k1, da_lam_q2=da_lam_q2, da_lam_k2=da_lam_k2,
             da_subln_g=da_subln_g, da_w_o=da_w_o, conv_w=conv_w, conv_b=conv_b, conv_ln_g=conv_ln_g,
             conv_ln_b=conv_ln_b, conv_w_o=conv_w_o, gla_gate_w2=gla_gate_w2, gla_gate_b=gla_gate_b,
             gla_norm_g=gla_norm_g, gla_w_o=gla_w_o, w_out=w_out, ln1_g=ln1_g, ln1_b=ln1_b, w_router=w_router,
             w_gate=w_gate, w_up=w_up, w_down=w_down, ln2_g=ln2_g, ln2_b=ln2_b)
    return _encode([x_prompt, x_sample], p, DEPTH)
```

```python
import functools
import math

import jax
import jax.numpy as jnp
from jax import lax
from jax.experimental import pallas as pl
from jax.experimental.pallas import tpu as pltpu

F32 = jnp.float32
BF16 = jnp.bfloat16
I32 = jnp.int32

D_MODEL = 1024
DEPTH = 4
DA_HEADS = 4
DA_HEAD_DIM = 64
DA_WIDTH = DA_HEADS * 2 * DA_HEAD_DIM
ROPE_THETA = 10000.0
CONV_WIDTH = 512
CONV_KERNEL = 31
GLA_HEADS = 4
GLA_DK = 64
GLA_DV = 128
GLA_KEY_WIDTH = GLA_HEADS * GLA_DK
GLA_VAL_WIDTH = GLA_HEADS * GLA_DV
GLA_GATE_RANK = 16
GLA_TAU = 16.0
GLA_CHUNK = 64
N_BRANCH = 3
N_EXPERTS = 16
EXPERT_FF = 2048
CAPACITY_FACTOR = 2
DEEPNORM_ALPHA = (2 * DEPTH) ** 0.25
NORM_EPS = 1e-5

LANES = 128
TOKEN_CHUNK = 256
SLOT_WINDOW = 256
SLOT_ALIGN = 16
CONV_HALO = 16
VMEM_LIMIT = 56 * 1024 * 1024

_NT = (((1,), (1,)), ((), ()))
_TN = (((0,), (0,)), ((), ()))


def _cparams(*sem):
    return pltpu.CompilerParams(dimension_semantics=sem, vmem_limit_bytes=VMEM_LIMIT)


def _dot(a, b):
    return jnp.dot(a, b, preferred_element_type=F32)


def _sigmoid(x):
    return 1.0 / (1.0 + jnp.exp(-x))


def _layernorm(y, g, b):
    mu = jnp.mean(y, axis=-1, keepdims=True)
    yc = y - mu
    var = jnp.mean(yc * yc, axis=-1, keepdims=True)
    return yc * lax.rsqrt(var + NORM_EPS) * g + b


def _proj_kernel(x_ref, cos_ref, sin_ref, wqk_ref, wv_ref, wcu_ref, wg4_ref, wlr_ref, gw_ref, gb_ref,
                 q_ref, k_ref, v_ref, u_ref, gq_ref, gk_ref, gv_ref, gr_ref, gf_ref, gbw_ref):
    xb = x_ref[...].astype(BF16)
    tm = xb.shape[0]
    qk = _dot(xb, wqk_ref[...])
    cos = cos_ref[...]
    sin = sin_ref[...]
    lane = lax.broadcasted_iota(I32, (tm, LANES), 1)
    first_half = (lane % DA_HEAD_DIM) < (DA_HEAD_DIM // 2)
    nqb = DA_WIDTH // LANES
    for cb in range(2 * nqb):
        xc = qk[:, cb * LANES:(cb + 1) * LANES]
        rot = jnp.where(first_half, pltpu.roll(xc, LANES - DA_HEAD_DIM // 2, 1), pltpu.roll(xc, DA_HEAD_DIM // 2, 1))
        r = xc * cos + rot * sin
        if cb < nqb:
            q_ref[:, cb * LANES:(cb + 1) * LANES] = (r * (DA_HEAD_DIM ** -0.5)).astype(BF16)
        else:
            k_ref[:, (cb - nqb) * LANES:(cb - nqb + 1) * LANES] = r.astype(BF16)
    v_ref[...] = _dot(xb, wv_ref[...]).astype(BF16)
    cu = _dot(xb, wcu_ref[...])
    u_ref[...] = cu[:, :CONV_WIDTH] * _sigmoid(cu[:, CONV_WIDTH:])
    g4 = _dot(xb, wg4_ref[...])
    kw, vw = GLA_KEY_WIDTH, GLA_VAL_WIDTH
    gq_ref[...] = g4[:, :kw] * (GLA_DK ** -0.5)
    gk_ref[...] = g4[:, kw:2 * kw]
    gv_ref[...] = g4[:, 2 * kw:2 * kw + vw]
    gr_ref[...] = g4[:, 2 * kw + vw:]
    lr = _dot(xb, wlr_ref[...])
    gp = _dot(lr.astype(BF16), gw_ref[...]) + gb_ref[...]
    lsig = jnp.minimum(gp, 0.0) - jnp.log(1.0 + jnp.exp(-jnp.abs(gp)))
    lsig = lsig * (1.0 / GLA_TAU)
    gf_ref[...] = lsig[:, :kw]
    gbw_ref[...] = lsig[:, kw:]


def _proj(x, cos, sin, w, seq, tm):
    nt = x.shape[0]
    nseq = seq // tm
    row = lambda i: (i, 0)
    const = lambda i: (0, 0)
    full = lambda a: pl.BlockSpec(a.shape, const)
    outs = [(DA_WIDTH, BF16), (DA_WIDTH, BF16), (DA_WIDTH, BF16), (CONV_WIDTH, F32),
            (GLA_KEY_WIDTH, F32), (GLA_KEY_WIDTH, F32), (GLA_VAL_WIDTH, F32), (GLA_VAL_WIDTH, F32),
            (GLA_KEY_WIDTH, F32), (GLA_KEY_WIDTH, F32)]
    return pl.pallas_call(
        _proj_kernel,
        grid=(nt // tm,),
        in_specs=[pl.BlockSpec((tm, D_MODEL), row),
                  pl.BlockSpec((tm, LANES), lambda i: (i % nseq, 0)),
                  pl.BlockSpec((tm, LANES), lambda i: (i % nseq, 0)),
                  full(w["wqk"]), full(w["wv"]), full(w["wcu"]), full(w["wg4"]), full(w["wlr"]),
                  full(w["gw"]), full(w["gb"])],
        out_specs=[pl.BlockSpec((tm, c), row) for c, _ in outs],
        out_shape=[jax.ShapeDtypeStruct((nt, c), d) for c, d in outs],
        compiler_params=_cparams("parallel"),
        name="proj",
    )(x, cos, sin, w["wqk"], w["wv"], w["wcu"], w["wg4"], w["wlr"], w["gw"], w["gb"])


def _attn_kernel(lamp_ref, g_ref, q_ref, k_ref, v_ref, o_ref, *, lam_init):
    lp = lamp_ref[...]
    lam = (jnp.exp(jnp.sum(lp[0:1] * lp[1:2], axis=1, keepdims=True))
           - jnp.exp(jnp.sum(lp[2:3] * lp[3:4], axis=1, keepdims=True)) + lam_init)
    q = q_ref[...]
    k = k_ref[...]
    lane = lax.broadcasted_iota(I32, q.shape, 1)
    zero = jnp.zeros_like(q)
    s0 = lax.dot_general(jnp.where(lane < DA_HEAD_DIM, q, zero), k, _NT, preferred_element_type=F32)
    s1 = lax.dot_general(jnp.where(lane >= DA_HEAD_DIM, q, zero), k, _NT, preferred_element_type=F32)
    e0 = jnp.exp(s0 - jnp.max(s0, axis=1, keepdims=True))
    e1 = jnp.exp(s1 - jnp.max(s1, axis=1, keepdims=True))
    r0 = 1.0 / jnp.sum(e0, axis=1, keepdims=True)
    r1 = lam / jnp.sum(e1, axis=1, keepdims=True)
    p = e0 * r0 - e1 * r1
    o = _dot(p.astype(BF16), v_ref[...])
    ms = jnp.mean(o * o, axis=1, keepdims=True)
    y = o * lax.rsqrt(ms + NORM_EPS) * g_ref[...] * (1.0 - lam_init)
    o_ref[...] = y.astype(BF16)


def _attn(q, k, v, lamp, subln_g, lam_init, nbatch, seq, tq):
    nt = q.shape[0]
    nq = seq // tq
    return pl.pallas_call(
        functools.partial(_attn_kernel, lam_init=lam_init),
        grid=(nbatch, DA_HEADS, nq),
        in_specs=[pl.BlockSpec(lamp.shape, lambda b, h, i: (0, 0)),
                  pl.BlockSpec(subln_g.shape, lambda b, h, i: (0, 0)),
                  pl.BlockSpec((tq, LANES), lambda b, h, i: (b * nq + i, h)),
                  pl.BlockSpec((seq, LANES), lambda b, h, i: (b, h)),
                  pl.BlockSpec((seq, LANES), lambda b, h, i: (b, h))],
        out_specs=pl.BlockSpec((tq, LANES), lambda b, h, i: (b * nq + i, h)),
        out_shape=jax.ShapeDtypeStruct((nt, DA_WIDTH), BF16),
        compiler_params=_cparams("parallel", "parallel", "parallel"),
        name="diff_attn",
    )(lamp, subln_g, q, k, v)


CONV_ROWS = 32


SUBLANES = 8


def _conv_kernel(prev_ref, cur_ref, next_ref, w_ref, cb_ref, lg_ref, lb_ref, o_ref, scr, shifted, *, ts, nblk):
    i = pl.program_id(1)
    scr[0:CONV_HALO, :] = jnp.where(i > 0, prev_ref[...], 0.0)
    scr[CONV_HALO:CONV_HALO + ts, :] = cur_ref[...]
    scr[CONV_HALO + ts:2 * CONV_HALO + ts, :] = jnp.where(i < nblk - 1, next_ref[...], 0.0)
    span = ts + 2 * CONV_HALO - SUBLANES
    for s in range(1, SUBLANES):
        shifted[s - 1] = scr[s:s + span, :]
    w = w_ref[...]
    base = CONV_HALO - CONV_KERNEL // 2
    for rb in range(ts // CONV_ROWS):
        r0 = rb * CONV_ROWS
        acc = jnp.zeros((CONV_ROWS, CONV_WIDTH), F32)
        for t in range(CONV_KERNEL):
            phase = (base + t) % SUBLANES
            row = r0 + base + t - phase
            if phase == 0:
                xs = scr[row:row + CONV_ROWS, :]
            else:
                xs = shifted[phase - 1, row:row + CONV_ROWS, :]
            acc = acc + w[t:t + 1, :] * xs
        h = _layernorm(acc + cb_ref[...], lg_ref[...], lb_ref[...])
        o_ref[r0:r0 + CONV_ROWS, :] = (h * _sigmoid(h)).astype(BF16)


def _conv(u, w, cb, lg, lb, nbatch, seq, ts):
    nt = u.shape[0]
    nblk = seq // ts
    hb = ts // CONV_HALO
    nhalo = nt // CONV_HALO
    const = lambda b, i: (0, 0)
    return pl.pallas_call(
        functools.partial(_conv_kernel, ts=ts, nblk=nblk),
        grid=(nbatch, nblk),
        in_specs=[pl.BlockSpec((CONV_HALO, CONV_WIDTH), lambda b, i: (jnp.maximum((b * nblk + i) * hb - 1, 0), 0)),
                  pl.BlockSpec((ts, CONV_WIDTH), lambda b, i: (b * nblk + i, 0)),
                  pl.BlockSpec((CONV_HALO, CONV_WIDTH),
                               lambda b, i: (jnp.minimum((b * nblk + i + 1) * hb, nhalo - 1), 0)),
                  pl.BlockSpec(w.shape, const), pl.BlockSpec(cb.shape, const),
                  pl.BlockSpec(lg.shape, const), pl.BlockSpec(lb.shape, const)],
        out_specs=pl.BlockSpec((ts, CONV_WIDTH), lambda b, i: (b * nblk + i, 0)),
        out_shape=jax.ShapeDtypeStruct((nt, CONV_WIDTH), BF16),
        scratch_shapes=[pltpu.VMEM((ts + 2 * CONV_HALO, CONV_WIDTH), F32),
                        pltpu.VMEM((SUBLANES - 1, ts + 2 * CONV_HALO - SUBLANES, CONV_WIDTH), F32)],
        compiler_params=_cparams("parallel", "parallel"),
        name="conformer_conv",
    )(u, u, u, w, cb, lg, lb)


def _split3(g):
    hi = g.astype(BF16)
    r1 = g - hi.astype(F32)
    mid = r1.astype(BF16)
    lo = (r1 - mid.astype(F32)).astype(BF16)
    return hi, mid, lo


def _gla_chunk(q_ref, k_ref, v_ref, g_ref, o_ref, st_ref, d, ci, reverse):
    c = GLA_CHUNK
    r = lax.broadcasted_iota(I32, (c, c), 0)
    cc = lax.broadcasted_iota(I32, (c, c), 1)
    keep = (cc >= r) if reverse else (cc <= r)
    tri = jnp.where(keep, 1.0, 0.0).astype(BF16)
    lane = lax.broadcasted_iota(I32, (c, LANES), 1)
    rows = slice(ci * c, (ci + 1) * c)
    hi, mid, lo = _split3(g_ref[rows, :])
    b = _dot(tri, hi) + _dot(tri, mid) + _dot(tri, lo)
    b_last = b[0:1, :] if reverse else b[c - 1:c, :]
    q = q_ref[rows, :]
    k = k_ref[rows, :]
    qt = (q * jnp.exp(b)).astype(BF16)
    kt = (k * jnp.exp(-b)).astype(BF16)
    kd = (k * jnp.exp(b_last - b)).astype(BF16)
    decay = jnp.exp(b_last)
    for h in range(2):
        in_head = (lane >= h * GLA_DK) & (lane < (h + 1) * GLA_DK)
        qm = jnp.where(in_head, qt, jnp.zeros_like(qt))
        a = lax.dot_general(qm, kt, _NT, preferred_element_type=F32)
        a = jnp.where(keep, a, 0.0).astype(BF16)
        vh = v_ref[rows, h * GLA_DV:(h + 1) * GLA_DV].astype(BF16)
        st = st_ref[d, h]
        o = _dot(a, vh) + lax.dot_general(qm, st.astype(BF16), _NT, preferred_element_type=F32)
        o_ref[rows, h * GLA_DV:(h + 1) * GLA_DV] = o
        kv = lax.dot_general(vh, kd, _TN, preferred_element_type=F32)
        st_ref[d, h] = st * decay + kv


def _gla_kernel(qf_ref, kf_ref, vf_ref, gf_ref, qb_ref, kb_ref, vb_ref, gb_ref, of_ref, ob_ref, st_ref, *, tc):
    @pl.when(pl.program_id(2) == 0)
    def _():
        st_ref[...] = jnp.zeros_like(st_ref)

    nchunk = tc // GLA_CHUNK
    for ci in range(nchunk):
        _gla_chunk(qf_ref, kf_ref, vf_ref, gf_ref, of_ref, st_ref, 0, ci, False)
        _gla_chunk(qb_ref, kb_ref, vb_ref, gb_ref, ob_ref, st_ref, 1, nchunk - 1 - ci, True)


def _gla(gq, gk, gv, gf, gb, nbatch, seq, tc):
    nt = gq.shape[0]
    nj = seq // tc
    fwd = lambda b, p, j: (b * nj + j, p)
    bwd = lambda b, p, j: (b * nj + nj - 1 - j, p)
    kspec = lambda m: pl.BlockSpec((tc, LANES), m)
    vspec = lambda m: pl.BlockSpec((tc, 2 * GLA_DV), m)
    return pl.pallas_call(
        functools.partial(_gla_kernel, tc=tc),
        grid=(nbatch, GLA_HEADS // 2, nj),
        in_specs=[kspec(fwd), kspec(fwd), vspec(fwd), kspec(fwd), kspec(bwd), kspec(bwd), vspec(bwd), kspec(bwd)],
        out_specs=[vspec(fwd), vspec(bwd)],
        out_shape=[jax.ShapeDtypeStruct((nt, GLA_VAL_WIDTH), F32)] * 2,
        scratch_shapes=[pltpu.VMEM((2, 2, GLA_DV, LANES), F32)],
        compiler_params=_cparams("parallel", "parallel", "arbitrary"),
        name="bi_gla",
    )(gq, gk, gv, gf, gq, gk, gv, gb)


def _merge_kernel(x_ref, ao_ref, ch_ref, of_ref, ob_ref, gr_ref, ng_ref, wa_ref, wc_ref, wl_ref, wmg_ref, wout_ref,
                  g1_ref, b1_ref, wr_ref, x1_ref, x1b_ref, aff_ref):
    x = x_ref[...]
    xb = x.astype(BF16)
    a = _dot(ao_ref[...], wa_ref[...])
    c = _dot(ch_ref[...], wc_ref[...])
    o = of_ref[...] + ob_ref[...]
    r = gr_ref[...]
    ng = ng_ref[...]
    parts = []
    for h in range(GLA_HEADS):
        oh = o[:, h * GLA_DV:(h + 1) * GLA_DV]
        rh = r[:, h * GLA_DV:(h + 1) * GLA_DV]
        yh = oh * lax.rsqrt(jnp.mean(oh * oh, axis=1, keepdims=True) + NORM_EPS) * ng
        parts.append((yh * (rh * _sigmoid(rh))).astype(BF16))
    l = _dot(jnp.concatenate(parts, axis=1), wl_ref[...])
    m = _sigmoid(_dot(xb, wmg_ref[:, 0:D_MODEL])) * a
    m = m + _sigmoid(_dot(xb, wmg_ref[:, D_MODEL:2 * D_MODEL])) * c
    m = m + _sigmoid(_dot(xb, wmg_ref[:, 2 * D_MODEL:3 * D_MODEL])) * l
    y = DEEPNORM_ALPHA * x + _dot(m.astype(BF16), wout_ref[...])
    x1 = _layernorm(y, g1_ref[...], b1_ref[...])
    x1_ref[...] = x1
    x1b = x1.astype(BF16)
    x1b_ref[...] = x1b
    logits = lax.dot_general(wr_ref[...], x1b, _NT, preferred_element_type=F32)
    ex = jnp.exp(logits - jnp.max(logits, axis=0, keepdims=True))
    aff_ref[...] = ex / jnp.sum(ex, axis=0, keepdims=True)


def _merge(x, ao, ch, of, ob, gr, w, tm):
    nt = x.shape[0]
    row = lambda i: (i, 0)
    const = lambda i: (0, 0)
    full = lambda a: pl.BlockSpec(a.shape, const)
    names = ["ng", "wa", "wc", "wl", "wmg", "wout", "g1", "b1", "wr"]
    return pl.pallas_call(
        _merge_kernel,
        grid=(nt // tm,),
        in_specs=[pl.BlockSpec((tm, D_MODEL), row), pl.BlockSpec((tm, DA_WIDTH), row),
                  pl.BlockSpec((tm, CONV_WIDTH), row), pl.BlockSpec((tm, GLA_VAL_WIDTH), row),
                  pl.BlockSpec((tm, GLA_VAL_WIDTH), row), pl.BlockSpec((tm, GLA_VAL_WIDTH), row)]
                 + [full(w[n]) for n in names],
        out_specs=[pl.BlockSpec((tm, D_MODEL), row), pl.BlockSpec((tm, D_MODEL), row),
                   pl.BlockSpec((N_EXPERTS, tm), lambda i: (0, i))],
        out_shape=[jax.ShapeDtypeStruct((nt, D_MODEL), F32), jax.ShapeDtypeStruct((nt, D_MODEL), BF16),
                   jax.ShapeDtypeStruct((N_EXPERTS, nt), F32)],
        compiler_params=_cparams("parallel"),
        name="merge_ln1_router",
    )(x, ao, ch, of, ob, gr, *[w[n] for n in names])


def _select_kernel(aff_ref, pos_ref, offs_ref, sel_ref, *, cap, slot_base, idx_bits):
    ne, n = aff_ref.shape
    bits = pltpu.bitcast(aff_ref[...], I32)
    capf = float(cap)

    def count(mask):
        return jnp.sum(jnp.where(mask, 1.0, 0.0), axis=1, keepdims=True)

    def value_step(i, t):
        cand = t | jnp.left_shift(jnp.int32(1), 30 - i)
        return jnp.where(count(bits >= cand) >= capf, cand, t)

    thr = lax.fori_loop(0, 31, value_step, jnp.zeros((ne, 1), I32))
    above = bits > thr
    tie = bits == thr
    need = capf - count(above)
    idx = lax.broadcasted_iota(I32, (ne, n), 1)

    def index_step(i, ans):
        cand = ans | jnp.left_shift(jnp.int32(1), idx_bits - 1 - i)
        return jnp.where(count(tie & (idx < cand)) < need, cand, ans)

    last_tie = lax.fori_loop(0, idx_bits, index_step, jnp.zeros((ne, 1), I32))
    sel_ref[...] = jnp.where(above | (tie & (idx <= last_tie)), 1.0, 0.0)

    ch = TOKEN_CHUNK
    upper = jnp.where(lax.broadcasted_iota(I32, (ch, ch), 0) <= lax.broadcasted_iota(I32, (ch, ch), 1),
                      1.0, 0.0).astype(BF16)
    lane = lax.broadcasted_iota(I32, (ne, LANES), 1)

    offs_ref[...] = jnp.zeros_like(offs_ref)

    def chunk_step(j, run):
        off = pl.multiple_of(j * ch, ch)
        m = sel_ref[:, pl.ds(off, ch)]
        incl = _dot(m.astype(BF16), upper)
        p = jnp.where(m > 0.0, run + incl + (slot_base - 1.0), -1.0)
        pos_ref[:, pl.ds(off, ch)] = p.astype(I32)
        offs_ref[...] = jnp.where(lane == j, jnp.broadcast_to(run, (ne, LANES)).astype(I32), offs_ref[...])
        return run + incl[:, ch - 1:ch]

    lax.fori_loop(0, n // ch, chunk_step, jnp.zeros((ne, 1), F32))


def _select(aff, cap, slot_base):
    ne, n = aff.shape
    assert n % TOKEN_CHUNK == 0 and n // TOKEN_CHUNK <= LANES
    return pl.pallas_call(
        functools.partial(_select_kernel, cap=cap, slot_base=slot_base, idx_bits=max(1, (n - 1).bit_length())),
        out_shape=[jax.ShapeDtypeStruct((ne, n), I32), jax.ShapeDtypeStruct((ne, LANES), I32)],
        scratch_shapes=[pltpu.VMEM((ne, n), F32)],
        compiler_params=pltpu.CompilerParams(vmem_limit_bytes=VMEM_LIMIT),
        name="expert_select",
    )(aff)


GATHER_RING = 4
GATE_ROWS = 16


def _ffn_kernel(blo_ref, bhi_ref, pos_ref, aff_ref, x_hbm, wg_ref, wu_ref, wd_ref, ye_ref, xbuf, sem, acc_ref,
                gate_ref, *, ts, sub):
    nj = pl.num_programs(1)
    j = pl.program_id(1)
    step = pl.program_id(0) * nj + j
    nsteps = pl.num_programs(0) * nj
    nsub = ts // sub
    ch = TOKEN_CHUNK
    ahead = GATHER_RING - 1

    def bounds(st):
        los = [blo_ref[st * nsub + t] for t in range(nsub)]
        ns = [bhi_ref[st * nsub + t] - los[t] for t in range(nsub)]
        return los, ns

    def chunk_of(los, ns, q):
        c = los[nsub - 1] + q - sum(ns[:nsub - 1])
        for t in range(nsub - 2, -1, -1):
            c = jnp.where(q < sum(ns[:t + 1]), los[t] + q - sum(ns[:t]), c)
        return c

    def copy(c, slot):
        return pltpu.make_async_copy(x_hbm.at[pl.ds(pl.multiple_of(c * ch, ch), ch), :], xbuf.at[slot], sem.at[slot])

    def issue_head(los, ns):
        for q in range(ahead):
            @pl.when(q < sum(ns))
            def _(q=q):
                copy(chunk_of(los, ns, q), q % GATHER_RING).start()

    los, ns = bounds(step)
    total = sum(ns)

    @pl.when(step == 0)
    def _():
        issue_head(los, ns)

    acc_ref[...] = jnp.zeros_like(acc_ref)
    gate_ref[...] = jnp.zeros_like(gate_ref)
    piece = lax.broadcasted_iota(I32, (GATE_ROWS, ch), 0)
    qbase = jnp.int32(0)
    for t in range(nsub):
        slot_id = lax.broadcasted_iota(I32, (sub, ch), 0) + (j * ts + t * sub)
        rows = slice(t * sub, (t + 1) * sub)

        def chunk_step(c, carry, t=t, slot_id=slot_id, rows=rows, qbase=qbase):
            q = qbase + c
            slot = q % GATHER_RING
            copy(0, slot).wait()

            @pl.when(q + ahead < total)
            def _():
                copy(chunk_of(los, ns, q + ahead), (q + ahead) % GATHER_RING).start()

            off = pl.multiple_of((los[t] + c) * ch, ch)
            onehot = jnp.where(pos_ref[:, pl.ds(off, ch)] == slot_id, 1.0, 0.0).astype(BF16)
            acc_ref[rows, :] += _dot(onehot, xbuf[slot])
            hi, mid, lo = _split3(aff_ref[:, pl.ds(off, ch)])
            pieces = jnp.where(piece == 0, hi.astype(F32), jnp.where(piece == 1, mid.astype(F32),
                               jnp.where(piece == 2, lo.astype(F32), 0.0))).astype(BF16)
            gate_ref[rows, :] += lax.dot_general(onehot, pieces, _NT, preferred_element_type=F32)
            return carry

        lax.fori_loop(0, ns[t], chunk_step, 0)
        qbase = qbase + ns[t]

    @pl.when(step + 1 < nsteps)
    def _():
        issue_head(*bounds(step + 1))

    xe = acc_ref[...].astype(BF16)
    g = _dot(xe, wg_ref[...])
    u = _dot(xe, wu_ref[...])
    h = (g * _sigmoid(g)) * u
    gate = jnp.sum(gate_ref[...], axis=1, keepdims=True)
    ye_ref[...] = (_dot(h.astype(BF16), wd_ref[...]) * gate).astype(BF16)


def _ffn(blo, bhi, pos3, aff3, x1b, wg, wu, wd, cap_total, ts, sub):
    ntile = cap_total // ts
    nt = x1b.shape[0]
    per_expert = lambda e, j, *_: (e, 0, 0)
    return pl.pallas_call(
        functools.partial(_ffn_kernel, ts=ts, sub=sub),
        grid_spec=pltpu.PrefetchScalarGridSpec(
            num_scalar_prefetch=2,
            grid=(N_EXPERTS, ntile),
            in_specs=[pl.BlockSpec((None, 1, nt), per_expert),
                      pl.BlockSpec((None, 1, nt), per_expert),
                      pl.BlockSpec(memory_space=pl.ANY),
                      pl.BlockSpec((None, D_MODEL, EXPERT_FF), per_expert),
                      pl.BlockSpec((None, D_MODEL, EXPERT_FF), per_expert),
                      pl.BlockSpec((None, EXPERT_FF, D_MODEL), per_expert)],
            out_specs=pl.BlockSpec((None, ts, D_MODEL), lambda e, j, *_: (e, j, 0)),
            scratch_shapes=[pltpu.VMEM((GATHER_RING, TOKEN_CHUNK, D_MODEL), BF16),
                            pltpu.SemaphoreType.DMA((GATHER_RING,)),
                            pltpu.VMEM((ts, D_MODEL), F32), pltpu.VMEM((ts, GATE_ROWS), F32)]),
        out_shape=jax.ShapeDtypeStruct((N_EXPERTS, cap_total, D_MODEL), BF16),
        compiler_params=_cparams("arbitrary", "arbitrary"),
        name="gather_expert_ffn",
    )(blo, bhi, pos3, aff3, x1b, wg, wu, wd)


def _combine_kernel(st_ref, nw_ref, x1_ref, post_ref, g2_ref, b2_ref, ye_hbm, *rest, cap_total, split):
    nout = 1 if split is None else 2
    out_refs, (ybuf, sem, xtra, sem_x, p_ref, acc_ref) = rest[:nout], rest[nout:]
    i = pl.program_id(0)
    ntiles = pl.num_programs(0)
    tm = x1_ref.shape[0]
    w = SLOT_WINDOW
    cur = i % 2

    def window(tile, e, k):
        lo = st_ref[tile * N_EXPERTS + e] + k * w
        return lo, pl.multiple_of(jnp.minimum(lo, cap_total - w), SLOT_ALIGN)

    def first_copy(tile, e, which):
        return pltpu.make_async_copy(ye_hbm.at[e, pl.ds(window(tile, e, 0)[1], w), :],
                                     ybuf.at[which, pl.ds(e * w, w), :], sem.at[which, e])

    @pl.when(i == 0)
    def _():
        for e in range(N_EXPERTS):
            first_copy(0, e, 0).start()

    @pl.when(i + 1 < ntiles)
    def _():
        for e in range(N_EXPERTS):
            first_copy(i + 1, e, 1 - cur).start()

    lane = lax.broadcasted_iota(I32, (tm, w), 1)

    def onehot(e, k):
        lo, start = window(i, e, k)
        pe = post_ref[:, e:e + 1]
        rel = jnp.where((pe >= lo) & (pe < start + w), pe - start, -1)
        return jnp.where(rel == lane, 1.0, 0.0).astype(BF16)

    for e in range(N_EXPERTS):
        p_ref[:, e * w:(e + 1) * w] = onehot(e, 0)
    for e in range(N_EXPERTS):
        first_copy(i, e, cur).wait()
    acc_ref[...] = _dot(p_ref[...], ybuf[cur])
    for e in range(N_EXPERTS):
        def extra(k, carry, e=e):
            cp = pltpu.make_async_copy(ye_hbm.at[e, pl.ds(window(i, e, k)[1], w), :], xtra, sem_x.at[0])
            cp.start()
            cp.wait()
            acc_ref[...] += _dot(onehot(e, k), xtra[...])
            return carry

        lax.fori_loop(1, nw_ref[i * N_EXPERTS + e], extra, 0)
    y = _layernorm(DEEPNORM_ALPHA * x1_ref[...] + acc_ref[...], g2_ref[...], b2_ref[...])
    if split is None:
        out_refs[0][...] = y
    else:
        @pl.when(i < split)
        def _():
            out_refs[0][...] = y

        @pl.when(i >= split)
        def _():
            out_refs[1][...] = y


def _combine(starts, nwin, x1, post, g2, b2, ye, tm, split=None):
    nt = x1.shape[0]
    ntiles = nt // tm
    cap_total = ye.shape[1]
    row = lambda i, *_: (i, 0)
    const = lambda i, *_: (0, 0)
    if split is None:
        out_specs = [pl.BlockSpec((tm, D_MODEL), row)]
        out_shape = [jax.ShapeDtypeStruct((nt, D_MODEL), F32)]
    else:
        out_specs = [pl.BlockSpec((tm, D_MODEL), lambda i, *_: (jnp.minimum(i, split - 1), 0)),
                     pl.BlockSpec((tm, D_MODEL), lambda i, *_: (jnp.maximum(i - split, 0), 0))]
        out_shape = [jax.ShapeDtypeStruct((split * tm, D_MODEL), F32),
                     jax.ShapeDtypeStruct(((ntiles - split) * tm, D_MODEL), F32)]
    return pl.pallas_call(
        functools.partial(_combine_kernel, cap_total=cap_total, split=split),
        grid_spec=pltpu.PrefetchScalarGridSpec(
            num_scalar_prefetch=2,
            grid=(ntiles,),
            in_specs=[pl.BlockSpec((tm, D_MODEL), row), pl.BlockSpec((tm, N_EXPERTS), row),
                      pl.BlockSpec(g2.shape, const), pl.BlockSpec(b2.shape, const),
                      pl.BlockSpec(memory_space=pl.ANY)],
            out_specs=out_specs,
            scratch_shapes=[pltpu.VMEM((2, N_EXPERTS * SLOT_WINDOW, D_MODEL), BF16),
                            pltpu.SemaphoreType.DMA((2, N_EXPERTS)),
                            pltpu.VMEM((SLOT_WINDOW, D_MODEL), BF16), pltpu.SemaphoreType.DMA((1,)),
                            pltpu.VMEM((tm, N_EXPERTS * SLOT_WINDOW), BF16), pltpu.VMEM((tm, D_MODEL), F32)]),
        out_shape=out_shape,
        compiler_params=_cparams("arbitrary"),
        name="combine_ln2",
    )(starts, nwin, x1, post, g2, b2, ye)


def _tile_bounds(offs_groups, caps, chunk_bases, ts):
    los, his = [], []
    for offs, cap, cbase in zip(offs_groups, caps, chunk_bases):
        nch = offs.shape[1]
        ends = jnp.concatenate([offs[:, 1:], jnp.full((N_EXPERTS, 1), cap, I32)], axis=1)
        s0 = (jnp.arange(cap // ts, dtype=I32) * ts)[None, :, None]
        los.append(cbase + jnp.sum((ends[:, None, :] <= s0).astype(I32), axis=2))
        his.append(cbase + jnp.sum((offs[:, None, :] < s0 + ts).astype(I32), axis=2))
        del nch
    return jnp.concatenate(los, axis=1).reshape(-1), jnp.concatenate(his, axis=1).reshape(-1)


def _window_bounds(offs_groups, caps, slot_bases, tm):
    starts, nwins = [], []
    per = tm // TOKEN_CHUNK
    for offs, cap, sbase in zip(offs_groups, caps, slot_bases):
        ends = jnp.concatenate([offs, jnp.full((N_EXPERTS, 1), cap, I32)], axis=1)
        first = ends[:, 0:-1:per] + sbase
        stop = ends[:, per::per] + sbase
        st = (first // SLOT_ALIGN) * SLOT_ALIGN
        nw = jnp.maximum((stop - st + SLOT_WINDOW - 1) // SLOT_WINDOW, 1)
        starts.append(st.T)
        nwins.append(nw.T)
    return jnp.concatenate(starts, axis=0).reshape(-1), jnp.concatenate(nwins, axis=0).reshape(-1)


def _prep_layer(l, p):
    bf = lambda a: a.astype(BF16)
    w_in = p["w_in"][l]
    c = 0
    cuts = {}
    for name, width in (("q", DA_WIDTH), ("k", DA_WIDTH), ("v", DA_WIDTH), ("cu", 2 * CONV_WIDTH),
                        ("g4", 2 * GLA_KEY_WIDTH + 2 * GLA_VAL_WIDTH), ("lr", 2 * GLA_GATE_RANK),
                        ("mg", N_BRANCH * D_MODEL)):
        cuts[name] = w_in[:, c:c + width]
        c += width
    gw2 = p["gla_gate_w2"][l]
    gw = jnp.zeros((LANES, 2 * GLA_KEY_WIDTH), F32)
    gw = gw.at[0:GLA_GATE_RANK, 0:GLA_KEY_WIDTH].set(gw2[0])
    gw = gw.at[GLA_GATE_RANK:2 * GLA_GATE_RANK, GLA_KEY_WIDTH:].set(gw2[1])
    lamp = jnp.zeros((8, LANES), F32)
    for r, nm in enumerate(("da_lam_q1", "da_lam_k1", "da_lam_q2", "da_lam_k2")):
        lamp = lamp.at[r, 0:DA_HEAD_DIM].set(p[nm][l])
    row = lambda a: a.reshape(1, -1)
    return {
        "wqk": bf(jnp.concatenate([cuts["q"], cuts["k"]], axis=1)), "wv": bf(cuts["v"]), "wcu": bf(cuts["cu"]),
        "wg4": bf(cuts["g4"]), "wlr": bf(jnp.pad(cuts["lr"], ((0, 0), (0, LANES - 2 * GLA_GATE_RANK)))),
        "gw": bf(gw), "gb": p["gla_gate_b"][l].reshape(1, -1),
        "lamp": lamp, "subln": row(p["da_subln_g"][l]),
        "conv_w": jnp.pad(p["conv_w"][l], ((0, 1), (0, 0))), "conv_b": row(p["conv_b"][l]),
        "conv_lg": row(p["conv_ln_g"][l]), "conv_lb": row(p["conv_ln_b"][l]),
        "ng": row(p["gla_norm_g"][l]), "wa": bf(p["da_w_o"][l]), "wc": bf(p["conv_w_o"][l]),
        "wl": bf(p["gla_w_o"][l]), "wmg": bf(cuts["mg"]), "wout": bf(p["w_out"][l]),
        "g1": row(p["ln1_g"][l]), "b1": row(p["ln1_b"][l]), "wr": bf(p["w_router"][l].T),
        "wgate": bf(p["w_gate"][l]), "wup": bf(p["w_up"][l]), "wdown": bf(p["w_down"][l]),
        "g2": row(p["ln2_g"][l]), "b2": row(p["ln2_b"][l]),
    }


def _rope_tables(seq):
    d = DA_HEAD_DIM
    inv = 1.0 / (ROPE_THETA ** (jnp.arange(0, d, 2, dtype=F32) / d))
    ang = jnp.arange(seq, dtype=F32)[:, None] * inv[None, :]
    c, s = jnp.cos(ang), jnp.sin(ang)
    cos = jnp.concatenate([c, c], axis=1)
    sin = jnp.concatenate([-s, s], axis=1)
    reps = LANES // d
    return jnp.tile(cos, (1, reps)), jnp.tile(sin, (1, reps))


def _tiles(seq, group_tokens, caps):
    g = functools.reduce(math.gcd, group_tokens)
    gc = functools.reduce(math.gcd, caps)
    return {
        "proj": min(512, seq), "attn": min(256, seq), "conv": min(256, seq), "gla": min(512, seq),
        "merge": min(512, seq), "ffn": min(512, gc), "gather": min(256, gc), "combine": min(512, g),
    }


def _encode(xs, p, depth):
    seq = xs[0].shape[1]
    nbatch = sum(x.shape[0] for x in xs)
    group_tokens = [x.shape[0] * seq for x in xs]
    caps = [CAPACITY_FACTOR * n // N_EXPERTS for n in group_tokens]
    slot_bases = [sum(caps[:i]) for i in range(len(caps))]
    token_bases = [sum(group_tokens[:i]) for i in range(len(caps))]
    chunk_bases = [t // TOKEN_CHUNK for t in token_bases]
    cap_total = sum(caps)
    t = _tiles(seq, group_tokens, caps)
    assert cap_total >= SLOT_WINDOW and all(c % t["ffn"] == 0 for c in caps)
    x = jnp.concatenate([x.reshape(-1, D_MODEL) for x in xs], axis=0)
    cos, sin = _rope_tables(seq)
    for l in range(depth):
        w = _prep_layer(l, p)
        lam_init = 0.8 - 0.6 * math.exp(-0.3 * l)
        q, k, v, u, gq, gk, gv, gr, gf, gb = _proj(x, cos, sin, w, seq, t["proj"])
        ao = _attn(q, k, v, w["lamp"], w["subln"], lam_init, nbatch, seq, t["attn"])
        ch = _conv(u, w["conv_w"], w["conv_b"], w["conv_lg"], w["conv_lb"], nbatch, seq, t["conv"])
        of, ob = _gla(gq, gk, gv, gf, gb, nbatch, seq, t["gla"])
        x1, x1b, aff = _merge(x, ao, ch, of, ob, gr, w, t["merge"])
        pos_g, offs_g = [], []
        for n, tb, cap, sb in zip(group_tokens, token_bases, caps, slot_bases):
            pos, offs = _select(aff[:, tb:tb + n], cap, sb)
            pos_g.append(pos)
            offs_g.append(offs[:, :n // TOKEN_CHUNK])
        pos = jnp.concatenate(pos_g, axis=1)
        blo, bhi = _tile_bounds(offs_g, caps, chunk_bases, t["gather"])
        ye = _ffn(blo, bhi, pos.reshape(N_EXPERTS, 1, -1), aff.reshape(N_EXPERTS, 1, -1), x1b,
                  w["wgate"], w["wup"], w["wdown"], cap_total, t["ffn"], t["gather"])
        starts, nwin = _window_bounds(offs_g, caps, slot_bases, t["combine"])
        split = group_tokens[0] // t["combine"] if l == depth - 1 else None
        res = _combine(starts, nwin, x1, pos.T, w["g2"], w["b2"], ye, t["combine"], split=split)
        x = res[0]
    return tuple(xo.reshape(xg.shape) for xo, xg in zip(res, xs))


def kernel(x_prompt, x_sample, w_in, da_lam_q1, da_lam_k1, da_lam_q2, da_lam_k2, da_subln_g, da_w_o, conv_w, conv_b, conv_ln_g, conv_ln_b, conv_w_o, gla_gate_w2, gla_gate_b, gla_norm_g, gla_w_o, w_out, ln1_g, ln1_b, w_router, w_gate, w_up, w_down, ln2_g, ln2_b):
    p = dict(w_in=w_in, da_lam_q1=da_lam_q1, da_lam_k1=da_lam_k1, da_lam_q2=da_lam_q2, da_lam_k2=da_lam_k2,
             da_subln_g=da_subln_g, da_w_o=da_w_o, conv_w=conv_w, conv_b=conv_b, conv_ln_g=conv_ln_g,
             conv_ln_b=conv_ln_b, conv_w_o=conv_w_o, gla_gate_w2=gla_gate_w2, gla_gate_b=gla_gate_b,
             gla_norm_g=gla_norm_g, gla_w_o=gla_w_o, w_out=w_out, ln1_g=ln1_g, ln1_b=ln1_b, w_router=w_router,
             w_gate=w_gate, w_up=w_up, w_down=w_down, ln2_g=ln2_g, ln2_b=ln2_b)
    return _encode([x_prompt, x_sample], p, DEPTH)
```

```python
import functools
import math

import jax
import jax.numpy as jnp
from jax import lax
from jax.experimental import pallas as pl
from jax.experimental.pallas import tpu as pltpu

F32 = jnp.float32
BF16 = jnp.bfloat16
I32 = jnp.int32

D_MODEL = 1024
DEPTH = 4
DA_HEADS = 4
DA_HEAD_DIM = 64
DA_WIDTH = DA_HEADS * 2 * DA_HEAD_DIM
ROPE_THETA = 10000.0
CONV_WIDTH = 512
CONV_KERNEL = 31
GLA_HEADS = 4
GLA_DK = 64
GLA_DV = 128
GLA_KEY_WIDTH = GLA_HEADS * GLA_DK
GLA_VAL_WIDTH = GLA_HEADS * GLA_DV
GLA_GATE_RANK = 16
GLA_TAU = 16.0
GLA_CHUNK = 64
N_BRANCH = 3
N_EXPERTS = 16
EXPERT_FF = 2048
CAPACITY_FACTOR = 2
DEEPNORM_ALPHA = (2 * DEPTH) ** 0.25
NORM_EPS = 1e-5
LOG2E = 1.4426950408889634

LANES = 128
TOKEN_CHUNK = 256
SLOT_WINDOW = 256
SLOT_ALIGN = 16
CONV_HALO = 16
VMEM_LIMIT = 56 * 1024 * 1024

_NT = (((1,), (1,)), ((), ()))
_TN = (((0,), (0,)), ((), ()))


def _cparams(*sem):
    return pltpu.CompilerParams(dimension_semantics=sem, vmem_limit_bytes=VMEM_LIMIT)


def _dot(a, b):
    return jnp.dot(a, b, preferred_element_type=F32)


def _sigmoid(x):
    return 1.0 / (1.0 + jnp.exp(-x))


def _layernorm(y, g, b):
    mu = jnp.mean(y, axis=-1, keepdims=True)
    yc = y - mu
    var = jnp.mean(yc * yc, axis=-1, keepdims=True)
    return yc * lax.rsqrt(var + NORM_EPS) * g + b


def _proj_kernel(x_ref, cos_ref, sin_ref, wqk_ref, wv_ref, wcu_ref, wg4_ref, wlr_ref, gw_ref, gb_ref,
                 q_ref, k_ref, v_ref, u_ref, gq_ref, gk_ref, gv_ref, gr_ref, gf_ref, gbw_ref):
    xb = x_ref[...].astype(BF16)
    tm = xb.shape[0]
    qk = _dot(xb, wqk_ref[...])
    cos = cos_ref[...]
    sin = sin_ref[...]
    lane = lax.broadcasted_iota(I32, (tm, LANES), 1)
    first_half = (lane % DA_HEAD_DIM) < (DA_HEAD_DIM // 2)
    nqb = DA_WIDTH // LANES
    for cb in range(2 * nqb):
        xc = qk[:, cb * LANES:(cb + 1) * LANES]
        rot = jnp.where(first_half, pltpu.roll(xc, LANES - DA_HEAD_DIM // 2, 1), pltpu.roll(xc, DA_HEAD_DIM // 2, 1))
        r = xc * cos + rot * sin
        if cb < nqb:
            q_ref[:, cb * LANES:(cb + 1) * LANES] = (r * (DA_HEAD_DIM ** -0.5 * LOG2E)).astype(BF16)
        else:
            k_ref[:, (cb - nqb) * LANES:(cb - nqb + 1) * LANES] = r.astype(BF16)
    v_ref[...] = _dot(xb, wv_ref[...]).astype(BF16)
    cu = _dot(xb, wcu_ref[...])
    u_ref[...] = cu[:, :CONV_WIDTH] * _sigmoid(cu[:, CONV_WIDTH:])
    g4 = _dot(xb, wg4_ref[...])
    kw, vw = GLA_KEY_WIDTH, GLA_VAL_WIDTH
    gq_ref[...] = g4[:, :kw] * (GLA_DK ** -0.5)
    gk_ref[...] = g4[:, kw:2 * kw]
    gv_ref[...] = g4[:, 2 * kw:2 * kw + vw]
    gr_ref[...] = g4[:, 2 * kw + vw:]
    lr = _dot(xb, wlr_ref[...])
    gp = _dot(lr.astype(BF16), gw_ref[...]) + gb_ref[...]
    lsig = jnp.minimum(gp, 0.0) - jnp.log(1.0 + jnp.exp(-jnp.abs(gp)))
    lsig = lsig * (1.0 / GLA_TAU)
    gf_ref[...] = lsig[:, :kw]
    gbw_ref[...] = lsig[:, kw:]


def _proj(x, cos, sin, w, seq, tm):
    nt = x.shape[0]
    nseq = seq // tm
    row = lambda i: (i, 0)
    const = lambda i: (0, 0)
    full = lambda a: pl.BlockSpec(a.shape, const)
    outs = [(DA_WIDTH, BF16), (DA_WIDTH, BF16), (DA_WIDTH, BF16), (CONV_WIDTH, F32),
            (GLA_KEY_WIDTH, F32), (GLA_KEY_WIDTH, F32), (GLA_VAL_WIDTH, F32), (GLA_VAL_WIDTH, F32),
            (GLA_KEY_WIDTH, F32), (GLA_KEY_WIDTH, F32)]
    return pl.pallas_call(
        _proj_kernel,
        grid=(nt // tm,),
        in_specs=[pl.BlockSpec((tm, D_MODEL), row),
                  pl.BlockSpec((tm, LANES), lambda i: (i % nseq, 0)),
                  pl.BlockSpec((tm, LANES), lambda i: (i % nseq, 0)),
                  full(w["wqk"]), full(w["wv"]), full(w["wcu"]), full(w["wg4"]), full(w["wlr"]),
                  full(w["gw"]), full(w["gb"])],
        out_specs=[pl.BlockSpec((tm, c), row) for c, _ in outs],
        out_shape=[jax.ShapeDtypeStruct((nt, c), d) for c, d in outs],
        compiler_params=_cparams("parallel"),
        name="proj",
    )(x, cos, sin, w["wqk"], w["wv"], w["wcu"], w["wg4"], w["wlr"], w["gw"], w["gb"])


def _attn_kernel(lamp_ref, g_ref, q_ref, k_ref, v_ref, o_ref, *, lam_init):
    lp = lamp_ref[...]
    lam = (jnp.exp(jnp.sum(lp[0:1] * lp[1:2], axis=1, keepdims=True))
           - jnp.exp(jnp.sum(lp[2:3] * lp[3:4], axis=1, keepdims=True)) + lam_init)
    q = q_ref[...]
    k = k_ref[...]
    lane = lax.broadcasted_iota(I32, q.shape, 1)
    zero = jnp.zeros_like(q)
    s0 = lax.dot_general(jnp.where(lane < DA_HEAD_DIM, q, zero), k, _NT, preferred_element_type=F32)
    s1 = lax.dot_general(jnp.where(lane >= DA_HEAD_DIM, q, zero), k, _NT, preferred_element_type=F32)
    e0 = jnp.exp2(s0 - jnp.max(s0, axis=1, keepdims=True))
    e1 = jnp.exp2(s1 - jnp.max(s1, axis=1, keepdims=True))
    r0 = 1.0 / jnp.sum(e0, axis=1, keepdims=True)
    r1 = lam / jnp.sum(e1, axis=1, keepdims=True)
    p = e0 * r0 - e1 * r1
    o = _dot(p.astype(BF16), v_ref[...])
    ms = jnp.mean(o * o, axis=1, keepdims=True)
    y = o * lax.rsqrt(ms + NORM_EPS) * g_ref[...] * (1.0 - lam_init)
    o_ref[...] = y.astype(BF16)


def _attn(q, k, v, lamp, subln_g, lam_init, nbatch, seq, tq):
    nt = q.shape[0]
    nq = seq // tq
    return pl.pallas_call(
        functools.partial(_attn_kernel, lam_init=lam_init),
        grid=(nbatch, DA_HEADS, nq),
        in_specs=[pl.BlockSpec(lamp.shape, lambda b, h, i: (0, 0)),
                  pl.BlockSpec(subln_g.shape, lambda b, h, i: (0, 0)),
                  pl.BlockSpec((tq, LANES), lambda b, h, i: (b * nq + i, h)),
                  pl.BlockSpec((seq, LANES), lambda b, h, i: (b, h)),
                  pl.BlockSpec((seq, LANES), lambda b, h, i: (b, h))],
        out_specs=pl.BlockSpec((tq, LANES), lambda b, h, i: (b * nq + i, h)),
        out_shape=jax.ShapeDtypeStruct((nt, DA_WIDTH), BF16),
        compiler_params=_cparams("parallel", "parallel", "parallel"),
        name="diff_attn",
    )(lamp, subln_g, q, k, v)


CONV_ROWS = 32


SUBLANES = 8


def _conv_kernel(prev_ref, cur_ref, next_ref, w_ref, cb_ref, lg_ref, lb_ref, o_ref, scr, shifted, *, ts, nblk):
    i = pl.program_id(1)
    scr[0:CONV_HALO, :] = jnp.where(i > 0, prev_ref[...], 0.0)
    scr[CONV_HALO:CONV_HALO + ts, :] = cur_ref[...]
    scr[CONV_HALO + ts:2 * CONV_HALO + ts, :] = jnp.where(i < nblk - 1, next_ref[...], 0.0)
    span = ts + 2 * CONV_HALO - SUBLANES
    for s in range(1, SUBLANES):
        shifted[s - 1] = scr[s:s + span, :]
    w = w_ref[...]
    base = CONV_HALO - CONV_KERNEL // 2
    for rb in range(ts // CONV_ROWS):
        r0 = rb * CONV_ROWS
        acc = jnp.zeros((CONV_ROWS, CONV_WIDTH), F32)
        for t in range(CONV_KERNEL):
            phase = (base + t) % SUBLANES
            row = r0 + base + t - phase
            if phase == 0:
                xs = scr[row:row + CONV_ROWS, :]
            else:
                xs = shifted[phase - 1, row:row + CONV_ROWS, :]
            acc = acc + w[t:t + 1, :] * xs
        h = _layernorm(acc + cb_ref[...], lg_ref[...], lb_ref[...])
        o_ref[r0:r0 + CONV_ROWS, :] = (h * _sigmoid(h)).astype(BF16)


def _conv(u, w, cb, lg, lb, nbatch, seq, ts):
    nt = u.shape[0]
    nblk = seq // ts
    hb = ts // CONV_HALO
    nhalo = nt // CONV_HALO
    const = lambda b, i: (0, 0)
    return pl.pallas_call(
        functools.partial(_conv_kernel, ts=ts, nblk=nblk),
        grid=(nbatch, nblk),
        in_specs=[pl.BlockSpec((CONV_HALO, CONV_WIDTH), lambda b, i: (jnp.maximum((b * nblk + i) * hb - 1, 0), 0)),
                  pl.BlockSpec((ts, CONV_WIDTH), lambda b, i: (b * nblk + i, 0)),
                  pl.BlockSpec((CONV_HALO, CONV_WIDTH),
                               lambda b, i: (jnp.minimum((b * nblk + i + 1) * hb, nhalo - 1), 0)),
                  pl.BlockSpec(w.shape, const), pl.BlockSpec(cb.shape, const),
                  pl.BlockSpec(lg.shape, const), pl.BlockSpec(lb.shape, const)],
        out_specs=pl.BlockSpec((ts, CONV_WIDTH), lambda b, i: (b * nblk + i, 0)),
        out_shape=jax.ShapeDtypeStruct((nt, CONV_WIDTH), BF16),
        scratch_shapes=[pltpu.VMEM((ts + 2 * CONV_HALO, CONV_WIDTH), F32),
                        pltpu.VMEM((SUBLANES - 1, ts + 2 * CONV_HALO - SUBLANES, CONV_WIDTH), F32)],
        compiler_params=_cparams("parallel", "parallel"),
        name="conformer_conv",
    )(u, u, u, w, cb, lg, lb)


def _split3(g):
    hi = g.astype(BF16)
    r1 = g - hi.astype(F32)
    mid = r1.astype(BF16)
    lo = (r1 - mid.astype(F32)).astype(BF16)
    return hi, mid, lo


def _gla_kernel(qf_ref, kf_ref, vf_ref, gf_ref, qb_ref, kb_ref, vb_ref, gb_ref, of_ref, ob_ref, st_ref, *, tc):
    @pl.when(pl.program_id(1) == 0)
    def _():
        st_ref[...] = jnp.zeros_like(st_ref)

    c = GLA_CHUNK
    nchunk = tc // c
    npair = GLA_HEADS // 2
    refs = ((qf_ref, kf_ref, vf_ref, gf_ref, of_ref), (qb_ref, kb_ref, vb_ref, gb_ref, ob_ref))
    r = lax.broadcasted_iota(I32, (c, c), 0)
    cc = lax.broadcasted_iota(I32, (c, c), 1)
    r2 = lax.broadcasted_iota(I32, (c, 2 * c), 0)
    c2 = lax.broadcasted_iota(I32, (c, 2 * c), 1) % c
    keep = (cc <= r, cc >= r)
    keep2 = (c2 <= r2, c2 >= r2)
    tri = tuple(jnp.where(m, 1.0, 0.0).astype(BF16) for m in keep)
    head0 = lax.broadcasted_iota(I32, (c, LANES), 1) < GLA_DK
    vhead0 = lax.broadcasted_iota(I32, (c, 2 * GLA_DV), 1) < GLA_DV
    same_head = ((lax.broadcasted_iota(I32, (2 * GLA_DV, LANES), 0) < GLA_DV)
                 == (lax.broadcasted_iota(I32, (2 * GLA_DV, LANES), 1) < GLA_DK))
    items = [(d, p, s) for s in range(nchunk) for p in range(npair) for d in range(2)]

    def where_(d, p, s):
        ci = s if d == 0 else nchunk - 1 - s
        return slice(ci * c, (ci + 1) * c), slice(p * LANES, (p + 1) * LANES), slice(p * 2 * GLA_DV, (p + 1) * 2 * GLA_DV)

    b3 = {}
    for it in items:
        d, p, s = it
        rows, kl, _ = where_(*it)
        b3[it] = _dot(tri[d], jnp.concatenate(_split3(refs[d][3][rows, kl]), axis=1))
    qt, decay, amat, kvm, vbd = {}, {}, {}, {}, {}
    for it in items:
        d, p, s = it
        rows, kl, vl = where_(*it)
        b = b3[it][:, :LANES] + b3[it][:, LANES:2 * LANES] + b3[it][:, 2 * LANES:]
        b_last = b[0:1, :] if d == 1 else b[c - 1:c, :]
        q = refs[d][0][rows, kl]
        k = refs[d][1][rows, kl]
        qt[it] = (q * jnp.exp(b)).astype(BF16)
        kt = (k * jnp.exp(-b)).astype(BF16)
        kd = (k * jnp.exp(b_last - b)).astype(BF16)
        decay[it] = jnp.exp(b_last)
        zk = jnp.zeros_like(kt)
        kstack = jnp.concatenate([jnp.where(head0, kt, zk), jnp.where(head0, zk, kt)], axis=0)
        a = lax.dot_general(qt[it], kstack, _NT, preferred_element_type=F32)
        amat[it] = jnp.where(keep2[d], a, 0.0).astype(BF16)
        v = refs[d][2][rows, vl].astype(BF16)
        zv = jnp.zeros_like(v)
        vbd[it] = jnp.concatenate([jnp.where(vhead0, v, zv), jnp.where(vhead0, zv, v)], axis=0)
        kv = lax.dot_general(v, kd, _TN, preferred_element_type=F32)
        kvm[it] = jnp.where(same_head, kv, 0.0)
    pre = {}
    for d in range(2):
        for p in range(npair):
            st = st_ref[d, p]
            for s in range(nchunk):
                it = (d, p, s)
                pre[it] = st.astype(BF16)
                st = st * decay[it] + kvm[it]
            st_ref[d, p] = st
    for it in items:
        d, p, s = it
        rows, _, vl = where_(*it)
        refs[d][4][rows, vl] = (_dot(amat[it], vbd[it])
                                + lax.dot_general(qt[it], pre[it], _NT, preferred_element_type=F32))


def _gla(gq, gk, gv, gf, gb, nbatch, seq, tc):
    nt = gq.shape[0]
    nj = seq // tc
    fwd = lambda b, j: (b * nj + j, 0)
    bwd = lambda b, j: (b * nj + nj - 1 - j, 0)
    kspec = lambda m: pl.BlockSpec((tc, GLA_KEY_WIDTH), m)
    vspec = lambda m: pl.BlockSpec((tc, GLA_VAL_WIDTH), m)
    return pl.pallas_call(
        functools.partial(_gla_kernel, tc=tc),
        grid=(nbatch, nj),
        in_specs=[kspec(fwd), kspec(fwd), vspec(fwd), kspec(fwd), kspec(bwd), kspec(bwd), vspec(bwd), kspec(bwd)],
        out_specs=[vspec(fwd), vspec(bwd)],
        out_shape=[jax.ShapeDtypeStruct((nt, GLA_VAL_WIDTH), F32)] * 2,
        scratch_shapes=[pltpu.VMEM((2, GLA_HEADS // 2, 2 * GLA_DV, LANES), F32)],
        compiler_params=_cparams("parallel", "arbitrary"),
        name="bi_gla",
    )(gq, gk, gv, gf, gq, gk, gv, gb)


def _merge_kernel(x_ref, ao_ref, ch_ref, of_ref, ob_ref, gr_ref, ng_ref, wa_ref, wc_ref, wl_ref, wmg_ref, wout_ref,
                  g1_ref, b1_ref, wr_ref, x1_ref, x1b_ref, aff_ref):
    x = x_ref[...]
    xb = x.astype(BF16)
    a = _dot(ao_ref[...], wa_ref[...])
    c = _dot(ch_ref[...], wc_ref[...])
    o = of_ref[...] + ob_ref[...]
    r = gr_ref[...]
    ng = ng_ref[...]
    parts = []
    for h in range(GLA_HEADS):
        oh = o[:, h * GLA_DV:(h + 1) * GLA_DV]
        rh = r[:, h * GLA_DV:(h + 1) * GLA_DV]
        yh = oh * lax.rsqrt(jnp.mean(oh * oh, axis=1, keepdims=True) + NORM_EPS) * ng
        parts.append((yh * (rh * _sigmoid(rh))).astype(BF16))
    l = _dot(jnp.concatenate(parts, axis=1), wl_ref[...])
    m = _sigmoid(_dot(xb, wmg_ref[:, 0:D_MODEL])) * a
    m = m + _sigmoid(_dot(xb, wmg_ref[:, D_MODEL:2 * D_MODEL])) * c
    m = m + _sigmoid(_dot(xb, wmg_ref[:, 2 * D_MODEL:3 * D_MODEL])) * l
    y = DEEPNORM_ALPHA * x + _dot(m.astype(BF16), wout_ref[...])
    x1 = _layernorm(y, g1_ref[...], b1_ref[...])
    x1_ref[...] = x1
    x1b = x1.astype(BF16)
    x1b_ref[...] = x1b
    logits = lax.dot_general(wr_ref[...], x1b, _NT, preferred_element_type=F32)
    ex = jnp.exp(logits - jnp.max(logits, axis=0, keepdims=True))
    aff_ref[...] = ex / jnp.sum(ex, axis=0, keepdims=True)


def _merge(x, ao, ch, of, ob, gr, w, tm):
    nt = x.shape[0]
    row = lambda i: (i, 0)
    const = lambda i: (0, 0)
    full = lambda a: pl.BlockSpec(a.shape, const)
    names = ["ng", "wa", "wc", "wl", "wmg", "wout", "g1", "b1", "wr"]
    return pl.pallas_call(
        _merge_kernel,
        grid=(nt // tm,),
        in_specs=[pl.BlockSpec((tm, D_MODEL), row), pl.BlockSpec((tm, DA_WIDTH), row),
                  pl.BlockSpec((tm, CONV_WIDTH), row), pl.BlockSpec((tm, GLA_VAL_WIDTH), row),
                  pl.BlockSpec((tm, GLA_VAL_WIDTH), row), pl.BlockSpec((tm, GLA_VAL_WIDTH), row)]
                 + [full(w[n]) for n in names],
        out_specs=[pl.BlockSpec((tm, D_MODEL), row), pl.BlockSpec((tm, D_MODEL), row),
                   pl.BlockSpec((N_EXPERTS, tm), lambda i: (0, i))],
        out_shape=[jax.ShapeDtypeStruct((nt, D_MODEL), F32), jax.ShapeDtypeStruct((nt, D_MODEL), BF16),
                   jax.ShapeDtypeStruct((N_EXPERTS, nt), F32)],
        compiler_params=_cparams("parallel"),
        name="merge_ln1_router",
    )(x, ao, ch, of, ob, gr, *[w[n] for n in names])


def _select_kernel(aff_ref, pos_ref, offs_ref, sel_ref, *, cap, slot_base, idx_bits):
    ne, n = aff_ref.shape
    bits = pltpu.bitcast(aff_ref[...], I32)
    capf = float(cap)

    def count(mask):
        return jnp.sum(jnp.where(mask, 1.0, 0.0), axis=1, keepdims=True)

    def value_step(i, t):
        cand = t | jnp.left_shift(jnp.int32(1), 30 - i)
        return jnp.where(count(bits >= cand) >= capf, cand, t)

    thr = lax.fori_loop(0, 31, value_step, jnp.zeros((ne, 1), I32))
    above = bits > thr
    tie = bits == thr
    need = capf - count(above)
    idx = lax.broadcasted_iota(I32, (ne, n), 1)

    def index_step(i, ans):
        cand = ans | jnp.left_shift(jnp.int32(1), idx_bits - 1 - i)
        return jnp.where(count(tie & (idx < cand)) < need, cand, ans)

    last_tie = lax.fori_loop(0, idx_bits, index_step, jnp.zeros((ne, 1), I32))
    sel_ref[...] = jnp.where(above | (tie & (idx <= last_tie)), 1.0, 0.0)

    ch = TOKEN_CHUNK
    upper = jnp.where(lax.broadcasted_iota(I32, (ch, ch), 0) <= lax.broadcasted_iota(I32, (ch, ch), 1),
                      1.0, 0.0).astype(BF16)
    lane = lax.broadcasted_iota(I32, (ne, LANES), 1)

    offs_ref[...] = jnp.zeros_like(offs_ref)

    def chunk_step(j, run):
        off = pl.multiple_of(j * ch, ch)
        m = sel_ref[:, pl.ds(off, ch)]
        incl = _dot(m.astype(BF16), upper)
        p = jnp.where(m > 0.0, run + incl + (slot_base - 1.0), -1.0)
        pos_ref[:, pl.ds(off, ch)] = p.astype(I32)
        offs_ref[...] = jnp.where(lane == j, jnp.broadcast_to(run, (ne, LANES)).astype(I32), offs_ref[...])
        return run + incl[:, ch - 1:ch]

    lax.fori_loop(0, n // ch, chunk_step, jnp.zeros((ne, 1), F32))


def _select(aff, cap, slot_base):
    ne, n = aff.shape
    assert n % TOKEN_CHUNK == 0 and n // TOKEN_CHUNK <= LANES
    return pl.pallas_call(
        functools.partial(_select_kernel, cap=cap, slot_base=slot_base, idx_bits=max(1, (n - 1).bit_length())),
        out_shape=[jax.ShapeDtypeStruct((ne, n), I32), jax.ShapeDtypeStruct((ne, LANES), I32)],
        scratch_shapes=[pltpu.VMEM((ne, n), F32)],
        compiler_params=pltpu.CompilerParams(vmem_limit_bytes=VMEM_LIMIT),
        name="expert_select",
    )(aff)


GATHER_RING = 4
GATE_ROWS = 16


def _ffn_kernel(blo_ref, bhi_ref, pos_ref, aff_ref, x_hbm, wg_ref, wu_ref, wd_ref, ye_ref, xbuf, sem, acc_ref,
                gate_ref, *, ts, sub):
    nj = pl.num_programs(1)
    j = pl.program_id(1)
    step = pl.program_id(0) * nj + j
    nsteps = pl.num_programs(0) * nj
    nsub = ts // sub
    ch = TOKEN_CHUNK
    ahead = GATHER_RING - 1

    def bounds(st):
        los = [blo_ref[st * nsub + t] for t in range(nsub)]
        ns = [bhi_ref[st * nsub + t] - los[t] for t in range(nsub)]
        return los, ns

    def chunk_of(los, ns, q):
        c = los[nsub - 1] + q - sum(ns[:nsub - 1])
        for t in range(nsub - 2, -1, -1):
            c = jnp.where(q < sum(ns[:t + 1]), los[t] + q - sum(ns[:t]), c)
        return c

    def copy(c, slot):
        return pltpu.make_async_copy(x_hbm.at[pl.ds(pl.multiple_of(c * ch, ch), ch), :], xbuf.at[slot], sem.at[slot])

    def issue_head(los, ns):
        for q in range(ahead):
            @pl.when(q < sum(ns))
            def _(q=q):
                copy(chunk_of(los, ns, q), q % GATHER_RING).start()

    los, ns = bounds(step)
    total = sum(ns)

    @pl.when(step == 0)
    def _():
        issue_head(los, ns)

    acc_ref[...] = jnp.zeros_like(acc_ref)
    gate_ref[...] = jnp.zeros_like(gate_ref)
    piece = lax.broadcasted_iota(I32, (GATE_ROWS, ch), 0)
    qbase = jnp.int32(0)
    for t in range(nsub):
        slot_id = lax.broadcasted_iota(I32, (sub, ch), 0) + (j * ts + t * sub)
        rows = slice(t * sub, (t + 1) * sub)

        def chunk_step(c, carry, t=t, slot_id=slot_id, rows=rows, qbase=qbase):
            q = qbase + c
            slot = q % GATHER_RING
            copy(0, slot).wait()

            @pl.when(q + ahead < total)
            def _():
                copy(chunk_of(los, ns, q + ahead), (q + ahead) % GATHER_RING).start()

            off = pl.multiple_of((los[t] + c) * ch, ch)
            onehot = jnp.where(pos_ref[:, pl.ds(off, ch)] == slot_id, 1.0, 0.0).astype(BF16)
            acc_ref[rows, :] += _dot(onehot, xbuf[slot])
            hi, mid, lo = _split3(aff_ref[:, pl.ds(off, ch)])
            pieces = jnp.where(piece == 0, hi.astype(F32), jnp.where(piece == 1, mid.astype(F32),
                               jnp.where(piece == 2, lo.astype(F32), 0.0))).astype(BF16)
            gate_ref[rows, :] += lax.dot_general(onehot, pieces, _NT, preferred_element_type=F32)
            return carry

        lax.fori_loop(0, ns[t], chunk_step, 0)
        qbase = qbase + ns[t]

    @pl.when(step + 1 < nsteps)
    def _():
        issue_head(*bounds(step + 1))

    xe = acc_ref[...].astype(BF16)
    g = _dot(xe, wg_ref[...])
    u = _dot(xe, wu_ref[...])
    h = (g * _sigmoid(g)) * u
    gate = jnp.sum(gate_ref[...], axis=1, keepdims=True)
    ye_ref[...] = (_dot(h.astype(BF16), wd_ref[...]) * gate).astype(BF16)


def _ffn(blo, bhi, pos3, aff3, x1b, wg, wu, wd, layer, cap_total, ts, sub):
    ntile = cap_total // ts
    nt = x1b.shape[0]
    per_expert = lambda e, j, *_: (e, 0, 0)
    layer_expert = lambda e, j, *_: (layer, e, 0, 0)
    return pl.pallas_call(
        functools.partial(_ffn_kernel, ts=ts, sub=sub),
        grid_spec=pltpu.PrefetchScalarGridSpec(
            num_scalar_prefetch=2,
            grid=(N_EXPERTS, ntile),
            in_specs=[pl.BlockSpec((None, 1, nt), per_expert),
                      pl.BlockSpec((None, 1, nt), per_expert),
                      pl.BlockSpec(memory_space=pl.ANY),
                      pl.BlockSpec((None, None, D_MODEL, EXPERT_FF), layer_expert),
                      pl.BlockSpec((None, None, D_MODEL, EXPERT_FF), layer_expert),
                      pl.BlockSpec((None, None, EXPERT_FF, D_MODEL), layer_expert)],
            out_specs=pl.BlockSpec((None, ts, D_MODEL), lambda e, j, *_: (e, j, 0)),
            scratch_shapes=[pltpu.VMEM((GATHER_RING, TOKEN_CHUNK, D_MODEL), BF16),
                            pltpu.SemaphoreType.DMA((GATHER_RING,)),
                            pltpu.VMEM((ts, D_MODEL), F32), pltpu.VMEM((ts, GATE_ROWS), F32)]),
        out_shape=jax.ShapeDtypeStruct((N_EXPERTS, cap_total, D_MODEL), BF16),
        compiler_params=_cparams("arbitrary", "arbitrary"),
        name="gather_expert_ffn",
    )(blo, bhi, pos3, aff3, x1b, wg, wu, wd)


def _combine_kernel(st_ref, nw_ref, x1_ref, post_ref, g2_ref, b2_ref, ye_hbm, *rest, cap_total, split):
    nout = 1 if split is None else 2
    out_refs, (ybuf, sem, xtra, sem_x, p_ref, acc_ref) = rest[:nout], rest[nout:]
    i = pl.program_id(0)
    ntiles = pl.num_programs(0)
    tm = x1_ref.shape[0]
    w = SLOT_WINDOW
    cur = i % 2

    def window(tile, e, k):
        lo = st_ref[tile * N_EXPERTS + e] + k * w
        return lo, pl.multiple_of(jnp.minimum(lo, cap_total - w), SLOT_ALIGN)

    def first_copy(tile, e, which):
        return pltpu.make_async_copy(ye_hbm.at[e, pl.ds(window(tile, e, 0)[1], w), :],
                                     ybuf.at[which, pl.ds(e * w, w), :], sem.at[which, e])

    @pl.when(i == 0)
    def _():
        for e in range(N_EXPERTS):
            first_copy(0, e, 0).start()

    @pl.when(i + 1 < ntiles)
    def _():
        for e in range(N_EXPERTS):
            first_copy(i + 1, e, 1 - cur).start()

    lane = lax.broadcasted_iota(I32, (tm, w), 1)

    def onehot(e, k):
        lo, start = window(i, e, k)
        pe = post_ref[:, e:e + 1]
        rel = jnp.where((pe >= lo) & (pe < start + w), pe - start, -1)
        return jnp.where(rel == lane, 1.0, 0.0).astype(BF16)

    for e in range(N_EXPERTS):
        p_ref[:, e * w:(e + 1) * w] = onehot(e, 0)
    for e in range(N_EXPERTS):
        first_copy(i, e, cur).wait()
    acc_ref[...] = _dot(p_ref[...], ybuf[cur])
    for e in range(N_EXPERTS):
        def extra(k, carry, e=e):
            cp = pltpu.make_async_copy(ye_hbm.at[e, pl.ds(window(i, e, k)[1], w), :], xtra, sem_x.at[0])
            cp.start()
            cp.wait()
            acc_ref[...] += _dot(onehot(e, k), xtra[...])
            return carry

        lax.fori_loop(1, nw_ref[i * N_EXPERTS + e], extra, 0)
    y = _layernorm(DEEPNORM_ALPHA * x1_ref[...] + acc_ref[...], g2_ref[...], b2_ref[...])
    if split is None:
        out_refs[0][...] = y
    else:
        @pl.when(i < split)
        def _():
            out_refs[0][...] = y

        @pl.when(i >= split)
        def _():
            out_refs[1][...] = y


def _combine(starts, nwin, x1, post, g2, b2, ye, tm, split=None):
    nt = x1.shape[0]
    ntiles = nt // tm
    cap_total = ye.shape[1]
    row = lambda i, *_: (i, 0)
    const = lambda i, *_: (0, 0)
    if split is None:
        out_specs = [pl.BlockSpec((tm, D_MODEL), row)]
        out_shape = [jax.ShapeDtypeStruct((nt, D_MODEL), F32)]
    else:
        out_specs = [pl.BlockSpec((tm, D_MODEL), lambda i, *_: (jnp.minimum(i, split - 1), 0)),
                     pl.BlockSpec((tm, D_MODEL), lambda i, *_: (jnp.maximum(i - split, 0), 0))]
        out_shape = [jax.ShapeDtypeStruct((split * tm, D_MODEL), F32),
                     jax.ShapeDtypeStruct(((ntiles - split) * tm, D_MODEL), F32)]
    return pl.pallas_call(
        functools.partial(_combine_kernel, cap_total=cap_total, split=split),
        grid_spec=pltpu.PrefetchScalarGridSpec(
            num_scalar_prefetch=2,
            grid=(ntiles,),
            in_specs=[pl.BlockSpec((tm, D_MODEL), row), pl.BlockSpec((tm, N_EXPERTS), row),
                      pl.BlockSpec(g2.shape, const), pl.BlockSpec(b2.shape, const),
                      pl.BlockSpec(memory_space=pl.ANY)],
            out_specs=out_specs,
            scratch_shapes=[pltpu.VMEM((2, N_EXPERTS * SLOT_WINDOW, D_MODEL), BF16),
                            pltpu.SemaphoreType.DMA((2, N_EXPERTS)),
                            pltpu.VMEM((SLOT_WINDOW, D_MODEL), BF16), pltpu.SemaphoreType.DMA((1,)),
                            pltpu.VMEM((tm, N_EXPERTS * SLOT_WINDOW), BF16), pltpu.VMEM((tm, D_MODEL), F32)]),
        out_shape=out_shape,
        compiler_params=_cparams("arbitrary"),
        name="combine_ln2",
    )(starts, nwin, x1, post, g2, b2, ye)


def _tile_bounds(offs_groups, caps, chunk_bases, ts):
    los, his = [], []
    for offs, cap, cbase in zip(offs_groups, caps, chunk_bases):
        nch = offs.shape[1]
        ends = jnp.concatenate([offs[:, 1:], jnp.full((N_EXPERTS, 1), cap, I32)], axis=1)
        s0 = (jnp.arange(cap // ts, dtype=I32) * ts)[None, :, None]
        los.append(cbase + jnp.sum((ends[:, None, :] <= s0).astype(I32), axis=2))
        his.append(cbase + jnp.sum((offs[:, None, :] < s0 + ts).astype(I32), axis=2))
        del nch
    return jnp.concatenate(los, axis=1).reshape(-1), jnp.concatenate(his, axis=1).reshape(-1)


def _window_bounds(offs_groups, caps, slot_bases, tm):
    starts, nwins = [], []
    per = tm // TOKEN_CHUNK
    for offs, cap, sbase in zip(offs_groups, caps, slot_bases):
        ends = jnp.concatenate([offs, jnp.full((N_EXPERTS, 1), cap, I32)], axis=1)
        first = ends[:, 0:-1:per] + sbase
        stop = ends[:, per::per] + sbase
        st = (first // SLOT_ALIGN) * SLOT_ALIGN
        nw = jnp.maximum((stop - st + SLOT_WINDOW - 1) // SLOT_WINDOW, 1)
        starts.append(st.T)
        nwins.append(nw.T)
    return jnp.concatenate(starts, axis=0).reshape(-1), jnp.concatenate(nwins, axis=0).reshape(-1)


def _prep_layer(l, p):
    bf = lambda a: a.astype(BF16)
    w_in = p["w_in"][l]
    c = 0
    cuts = {}
    for name, width in (("q", DA_WIDTH), ("k", DA_WIDTH), ("v", DA_WIDTH), ("cu", 2 * CONV_WIDTH),
                        ("g4", 2 * GLA_KEY_WIDTH + 2 * GLA_VAL_WIDTH), ("lr", 2 * GLA_GATE_RANK),
                        ("mg", N_BRANCH * D_MODEL)):
        cuts[name] = w_in[:, c:c + width]
        c += width
    gw2 = p["gla_gate_w2"][l]
    gw = jnp.zeros((LANES, 2 * GLA_KEY_WIDTH), F32)
    gw = gw.at[0:GLA_GATE_RANK, 0:GLA_KEY_WIDTH].set(gw2[0])
    gw = gw.at[GLA_GATE_RANK:2 * GLA_GATE_RANK, GLA_KEY_WIDTH:].set(gw2[1])
    lamp = jnp.zeros((8, LANES), F32)
    for r, nm in enumerate(("da_lam_q1", "da_lam_k1", "da_lam_q2", "da_lam_k2")):
        lamp = lamp.at[r, 0:DA_HEAD_DIM].set(p[nm][l])
    row = lambda a: a.reshape(1, -1)
    return {
        "wqk": bf(jnp.concatenate([cuts["q"], cuts["k"]], axis=1)), "wv": bf(cuts["v"]), "wcu": bf(cuts["cu"]),
        "wg4": bf(cuts["g4"]), "wlr": bf(jnp.pad(cuts["lr"], ((0, 0), (0, LANES - 2 * GLA_GATE_RANK)))),
        "gw": bf(gw), "gb": p["gla_gate_b"][l].reshape(1, -1),
        "lamp": lamp, "subln": row(p["da_subln_g"][l]),
        "conv_w": jnp.pad(p["conv_w"][l], ((0, 1), (0, 0))), "conv_b": row(p["conv_b"][l]),
        "conv_lg": row(p["conv_ln_g"][l]), "conv_lb": row(p["conv_ln_b"][l]),
        "ng": row(p["gla_norm_g"][l]), "wa": bf(p["da_w_o"][l]), "wc": bf(p["conv_w_o"][l]),
        "wl": bf(p["gla_w_o"][l]), "wmg": bf(cuts["mg"]), "wout": bf(p["w_out"][l]),
        "g1": row(p["ln1_g"][l]), "b1": row(p["ln1_b"][l]), "wr": bf(p["w_router"][l].T),
        "g2": row(p["ln2_g"][l]), "b2": row(p["ln2_b"][l]),
    }


def _rope_tables(seq):
    d = DA_HEAD_DIM
    inv = 1.0 / (ROPE_THETA ** (jnp.arange(0, d, 2, dtype=F32) / d))
    ang = jnp.arange(seq, dtype=F32)[:, None] * inv[None, :]
    c, s = jnp.cos(ang), jnp.sin(ang)
    cos = jnp.concatenate([c, c], axis=1)
    sin = jnp.concatenate([-s, s], axis=1)
    reps = LANES // d
    return jnp.tile(cos, (1, reps)), jnp.tile(sin, (1, reps))


def _tiles(seq, group_tokens, caps):
    g = functools.reduce(math.gcd, group_tokens)
    gc = functools.reduce(math.gcd, caps)
    return {
        "proj": min(512, seq), "attn": min(256, seq), "conv": min(256, seq), "gla": min(512, seq),
        "merge": min(512, seq), "ffn": min(512, gc), "gather": min(256, gc), "combine": min(512, g),
    }


def _encode(xs, p, depth):
    seq = xs[0].shape[1]
    nbatch = sum(x.shape[0] for x in xs)
    group_tokens = [x.shape[0] * seq for x in xs]
    caps = [CAPACITY_FACTOR * n // N_EXPERTS for n in group_tokens]
    slot_bases = [sum(caps[:i]) for i in range(len(caps))]
    token_bases = [sum(group_tokens[:i]) for i in range(len(caps))]
    chunk_bases = [t // TOKEN_CHUNK for t in token_bases]
    cap_total = sum(caps)
    t = _tiles(seq, group_tokens, caps)
    assert cap_total >= SLOT_WINDOW and all(c % t["ffn"] == 0 for c in caps)
    x = jnp.concatenate([x.reshape(-1, D_MODEL) for x in xs], axis=0)
    cos, sin = _rope_tables(seq)
    wgate, wup, wdown = (p[n].astype(BF16) for n in ("w_gate", "w_up", "w_down"))
    for l in range(depth):
        w = _prep_layer(l, p)
        lam_init = 0.8 - 0.6 * math.exp(-0.3 * l)
        q, k, v, u, gq, gk, gv, gr, gf, gb = _proj(x, cos, sin, w, seq, t["proj"])
        ao = _attn(q, k, v, w["lamp"], w["subln"], lam_init, nbatch, seq, t["attn"])
        ch = _conv(u, w["conv_w"], w["conv_b"], w["conv_lg"], w["conv_lb"], nbatch, seq, t["conv"])
        of, ob = _gla(gq, gk, gv, gf, gb, nbatch, seq, t["gla"])
        x1, x1b, aff = _merge(x, ao, ch, of, ob, gr, w, t["merge"])
        pos_g, offs_g = [], []
        for n, tb, cap, sb in zip(group_tokens, token_bases, caps, slot_bases):
            pos, offs = _select(aff[:, tb:tb + n], cap, sb)
            pos_g.append(pos)
            offs_g.append(offs[:, :n // TOKEN_CHUNK])
        pos = jnp.concatenate(pos_g, axis=1)
        blo, bhi = _tile_bounds(offs_g, caps, chunk_bases, t["gather"])
        ye = _ffn(blo, bhi, pos.reshape(N_EXPERTS, 1, -1), aff.reshape(N_EXPERTS, 1, -1), x1b,
                  wgate, wup, wdown, l, cap_total, t["ffn"], t["gather"])
        starts, nwin = _window_bounds(offs_g, caps, slot_bases, t["combine"])
        split = group_tokens[0] // t["combine"] if l == depth - 1 else None
        res = _combine(starts, nwin, x1, pos.T, w["g2"], w["b2"], ye, t["combine"], split=split)
        x = res[0]
    return tuple(xo.reshape(xg.shape) for xo, xg in zip(res, xs))


def kernel(x_prompt, x_sample, w_in, da_lam_q1, da_lam_k1, da_lam_q2, da_lam_k2, da_subln_g, da_w_o, conv_w, conv_b, conv_ln_g, conv_ln_b, conv_w_o, gla_gate_w2, gla_gate_b, gla_norm_g, gla_w_o, w_out, ln1_g, ln1_b, w_router, w_gate, w_up, w_down, ln2_g, ln2_b):
    p = dict(w_in=w_in, da_lam_q1=da_lam_q1, da_lam_k1=da_lam_k1, da_lam_q2=da_lam_q2, da_lam_k2=da_lam_k2,
             da_subln_g=da_subln_g, da_w_o=da_w_o, conv_w=conv_w, conv_b=conv_b, conv_ln_g=conv_ln_g,
             conv_ln_b=conv_ln_b, conv_w_o=conv_w_o, gla_gate_w2=gla_gate_w2, gla_gate_b=gla_gate_b,
             gla_norm_g=gla_norm_g, gla_w_o=gla_w_o, w_out=w_out, ln1_g=ln1_g, ln1_b=ln1_b, w_router=w_router,
             w_gate=w_gate, w_up=w_up, w_down=w_down, ln2_g=ln2_g, ln2_b=ln2_b)
    return _encode([x_prompt, x_sample], p, DEPTH)
```

```python
import functools
import math

import jax
import jax.numpy as jnp
from jax import lax
from jax.experimental import pallas as pl
from jax.experimental.pallas import tpu as pltpu

F32 = jnp.float32
BF16 = jnp.bfloat16
I32 = jnp.int32

D_MODEL = 1024
DEPTH = 4
DA_HEADS = 4
DA_HEAD_DIM = 64
DA_WIDTH = DA_HEADS * 2 * DA_HEAD_DIM
ROPE_THETA = 10000.0
CONV_WIDTH = 512
CONV_KERNEL = 31
GLA_HEADS = 4
GLA_DK = 64
GLA_DV = 128
GLA_KEY_WIDTH = GLA_HEADS * GLA_DK
GLA_VAL_WIDTH = GLA_HEADS * GLA_DV
GLA_GATE_RANK = 16
GLA_TAU = 16.0
GLA_CHUNK = 64
N_BRANCH = 3
N_EXPERTS = 16
EXPERT_FF = 2048
CAPACITY_FACTOR = 2
DEEPNORM_ALPHA = (2 * DEPTH) ** 0.25
NORM_EPS = 1e-5
LOG2E = 1.4426950408889634

LANES = 128
TOKEN_CHUNK = 256
SLOT_WINDOW = 256
SLOT_ALIGN = 16
CONV_HALO = 16
VMEM_LIMIT = 56 * 1024 * 1024

_NT = (((1,), (1,)), ((), ()))
_TN = (((0,), (0,)), ((), ()))


def _cparams(*sem):
    return pltpu.CompilerParams(dimension_semantics=sem, vmem_limit_bytes=VMEM_LIMIT)


def _dot(a, b):
    return jnp.dot(a, b, preferred_element_type=F32)


def _sigmoid(x):
    return 1.0 / (1.0 + jnp.exp(-x))


def _layernorm(y, g, b):
    mu = jnp.mean(y, axis=-1, keepdims=True)
    yc = y - mu
    var = jnp.mean(yc * yc, axis=-1, keepdims=True)
    return yc * lax.rsqrt(var + NORM_EPS) * g + b


def _proj_kernel(x_ref, cos_ref, sin_ref, wqk_ref, wv_ref, wcu_ref, wg4_ref, wlr_ref, gw_ref, gb_ref,
                 q_ref, k_ref, v_ref, u_ref, gq_ref, gk_ref, gv_ref, gr_ref, gf_ref, gbw_ref):
    xb = x_ref[...].astype(BF16)
    tm = xb.shape[0]
    qk = _dot(xb, wqk_ref[...])
    cos = cos_ref[...]
    sin = sin_ref[...]
    lane = lax.broadcasted_iota(I32, (tm, LANES), 1)
    first_half = (lane % DA_HEAD_DIM) < (DA_HEAD_DIM // 2)
    nqb = DA_WIDTH // LANES
    for cb in range(2 * nqb):
        xc = qk[:, cb * LANES:(cb + 1) * LANES]
        rot = jnp.where(first_half, pltpu.roll(xc, LANES - DA_HEAD_DIM // 2, 1), pltpu.roll(xc, DA_HEAD_DIM // 2, 1))
        r = xc * cos + rot * sin
        if cb < nqb:
            q_ref[:, cb * LANES:(cb + 1) * LANES] = (r * (DA_HEAD_DIM ** -0.5 * LOG2E)).astype(BF16)
        else:
            k_ref[:, (cb - nqb) * LANES:(cb - nqb + 1) * LANES] = r.astype(BF16)
    v_ref[...] = _dot(xb, wv_ref[...]).astype(BF16)
    cu = _dot(xb, wcu_ref[...])
    u_ref[...] = cu[:, :CONV_WIDTH] * _sigmoid(cu[:, CONV_WIDTH:])
    g4 = _dot(xb, wg4_ref[...])
    kw, vw = GLA_KEY_WIDTH, GLA_VAL_WIDTH
    gq_ref[...] = g4[:, :kw] * (GLA_DK ** -0.5)
    gk_ref[...] = g4[:, kw:2 * kw]
    gv_ref[...] = g4[:, 2 * kw:2 * kw + vw]
    gr_ref[...] = g4[:, 2 * kw + vw:]
    lr = _dot(xb, wlr_ref[...])
    gp = _dot(lr.astype(BF16), gw_ref[...]) + gb_ref[...]
    lsig = jnp.minimum(gp, 0.0) - jnp.log(1.0 + jnp.exp(-jnp.abs(gp)))
    lsig = lsig * (1.0 / GLA_TAU)
    gf_ref[...] = lsig[:, :kw]
    gbw_ref[...] = lsig[:, kw:]


def _proj(x, cos, sin, w, seq, tm):
    nt = x.shape[0]
    nseq = seq // tm
    row = lambda i: (i, 0)
    const = lambda i: (0, 0)
    full = lambda a: pl.BlockSpec(a.shape, const)
    outs = [(DA_WIDTH, BF16), (DA_WIDTH, BF16), (DA_WIDTH, BF16), (CONV_WIDTH, F32),
            (GLA_KEY_WIDTH, F32), (GLA_KEY_WIDTH, F32), (GLA_VAL_WIDTH, F32), (GLA_VAL_WIDTH, F32),
            (GLA_KEY_WIDTH, F32), (GLA_KEY_WIDTH, F32)]
    return pl.pallas_call(
        _proj_kernel,
        grid=(nt // tm,),
        in_specs=[pl.BlockSpec((tm, D_MODEL), row),
                  pl.BlockSpec((tm, LANES), lambda i: (i % nseq, 0)),
                  pl.BlockSpec((tm, LANES), lambda i: (i % nseq, 0)),
                  full(w["wqk"]), full(w["wv"]), full(w["wcu"]), full(w["wg4"]), full(w["wlr"]),
                  full(w["gw"]), full(w["gb"])],
        out_specs=[pl.BlockSpec((tm, c), row) for c, _ in outs],
        out_shape=[jax.ShapeDtypeStruct((nt, c), d) for c, d in outs],
        compiler_params=_cparams("parallel"),
        name="proj",
    )(x, cos, sin, w["wqk"], w["wv"], w["wcu"], w["wg4"], w["wlr"], w["gw"], w["gb"])


def _attn_kernel(lamp_ref, g_ref, q_ref, k_ref, v_ref, o_ref, *, lam_init):
    lp = lamp_ref[...]
    lam = (jnp.exp(jnp.sum(lp[0:1] * lp[1:2], axis=1, keepdims=True))
           - jnp.exp(jnp.sum(lp[2:3] * lp[3:4], axis=1, keepdims=True)) + lam_init)
    for h in range(DA_HEADS):
        cols = slice(h * LANES, (h + 1) * LANES)
        q = q_ref[:, cols]
        k = k_ref[:, cols]
        lane = lax.broadcasted_iota(I32, q.shape, 1)
        zero = jnp.zeros_like(q)
        s0 = lax.dot_general(jnp.where(lane < DA_HEAD_DIM, q, zero), k, _NT, preferred_element_type=F32)
        s1 = lax.dot_general(jnp.where(lane >= DA_HEAD_DIM, q, zero), k, _NT, preferred_element_type=F32)
        e0 = jnp.exp2(s0 - jnp.max(s0, axis=1, keepdims=True))
        e1 = jnp.exp2(s1 - jnp.max(s1, axis=1, keepdims=True))
        r0 = 1.0 / jnp.sum(e0, axis=1, keepdims=True)
        r1 = lam / jnp.sum(e1, axis=1, keepdims=True)
        p = e0 * r0 - e1 * r1
        o = _dot(p.astype(BF16), v_ref[:, cols])
        ms = jnp.mean(o * o, axis=1, keepdims=True)
        y = o * lax.rsqrt(ms + NORM_EPS) * g_ref[...] * (1.0 - lam_init)
        o_ref[:, cols] = y.astype(BF16)


def _attn(q, k, v, lamp, subln_g, lam_init, nbatch, seq, tq):
    nt = q.shape[0]
    nq = seq // tq
    return pl.pallas_call(
        functools.partial(_attn_kernel, lam_init=lam_init),
        grid=(nbatch, nq),
        in_specs=[pl.BlockSpec(lamp.shape, lambda b, i: (0, 0)),
                  pl.BlockSpec(subln_g.shape, lambda b, i: (0, 0)),
                  pl.BlockSpec((tq, DA_WIDTH), lambda b, i: (b * nq + i, 0)),
                  pl.BlockSpec((seq, DA_WIDTH), lambda b, i: (b, 0)),
                  pl.BlockSpec((seq, DA_WIDTH), lambda b, i: (b, 0))],
        out_specs=pl.BlockSpec((tq, DA_WIDTH), lambda b, i: (b * nq + i, 0)),
        out_shape=jax.ShapeDtypeStruct((nt, DA_WIDTH), BF16),
        compiler_params=_cparams("parallel", "parallel"),
        name="diff_attn",
    )(lamp, subln_g, q, k, v)


CONV_ROWS = 32


SUBLANES = 8


def _conv_kernel(prev_ref, cur_ref, next_ref, w_ref, cb_ref, lg_ref, lb_ref, o_ref, scr, shifted, *, ts, nblk):
    i = pl.program_id(1)
    scr[0:CONV_HALO, :] = jnp.where(i > 0, prev_ref[...], 0.0)
    scr[CONV_HALO:CONV_HALO + ts, :] = cur_ref[...]
    scr[CONV_HALO + ts:2 * CONV_HALO + ts, :] = jnp.where(i < nblk - 1, next_ref[...], 0.0)
    span = ts + 2 * CONV_HALO - SUBLANES
    for s in range(1, SUBLANES):
        shifted[s - 1] = scr[s:s + span, :]
    w = w_ref[...]
    base = CONV_HALO - CONV_KERNEL // 2
    for rb in range(ts // CONV_ROWS):
        r0 = rb * CONV_ROWS
        acc = jnp.zeros((CONV_ROWS, CONV_WIDTH), F32)
        for t in range(CONV_KERNEL):
            phase = (base + t) % SUBLANES
            row = r0 + base + t - phase
            if phase == 0:
                xs = scr[row:row + CONV_ROWS, :]
            else:
                xs = shifted[phase - 1, row:row + CONV_ROWS, :]
            acc = acc + w[t:t + 1, :] * xs
        h = _layernorm(acc + cb_ref[...], lg_ref[...], lb_ref[...])
        o_ref[r0:r0 + CONV_ROWS, :] = (h * _sigmoid(h)).astype(BF16)


def _conv(u, w, cb, lg, lb, nbatch, seq, ts):
    nt = u.shape[0]
    nblk = seq // ts
    hb = ts // CONV_HALO
    nhalo = nt // CONV_HALO
    const = lambda b, i: (0, 0)
    return pl.pallas_call(
        functools.partial(_conv_kernel, ts=ts, nblk=nblk),
        grid=(nbatch, nblk),
        in_specs=[pl.BlockSpec((CONV_HALO, CONV_WIDTH), lambda b, i: (jnp.maximum((b * nblk + i) * hb - 1, 0), 0)),
                  pl.BlockSpec((ts, CONV_WIDTH), lambda b, i: (b * nblk + i, 0)),
                  pl.BlockSpec((CONV_HALO, CONV_WIDTH),
                               lambda b, i: (jnp.minimum((b * nblk + i + 1) * hb, nhalo - 1), 0)),
                  pl.BlockSpec(w.shape, const), pl.BlockSpec(cb.shape, const),
                  pl.BlockSpec(lg.shape, const), pl.BlockSpec(lb.shape, const)],
        out_specs=pl.BlockSpec((ts, CONV_WIDTH), lambda b, i: (b * nblk + i, 0)),
        out_shape=jax.ShapeDtypeStruct((nt, CONV_WIDTH), BF16),
        scratch_shapes=[pltpu.VMEM((ts + 2 * CONV_HALO, CONV_WIDTH), F32),
                        pltpu.VMEM((SUBLANES - 1, ts + 2 * CONV_HALO - SUBLANES, CONV_WIDTH), F32)],
        compiler_params=_cparams("parallel", "parallel"),
        name="conformer_conv",
    )(u, u, u, w, cb, lg, lb)


def _split3(g):
    hi = g.astype(BF16)
    r1 = g - hi.astype(F32)
    mid = r1.astype(BF16)
    lo = (r1 - mid.astype(F32)).astype(BF16)
    return hi, mid, lo


def _gla_kernel(qf_ref, kf_ref, vf_ref, gf_ref, qb_ref, kb_ref, vb_ref, gb_ref, of_ref, ob_ref, st_ref, *, tc):
    @pl.when(pl.program_id(1) == 0)
    def _():
        st_ref[...] = jnp.zeros_like(st_ref)

    c = GLA_CHUNK
    nchunk = tc // c
    npair = GLA_HEADS // 2
    refs = ((qf_ref, kf_ref, vf_ref, gf_ref, of_ref), (qb_ref, kb_ref, vb_ref, gb_ref, ob_ref))
    r = lax.broadcasted_iota(I32, (c, c), 0)
    cc = lax.broadcasted_iota(I32, (c, c), 1)
    r2 = lax.broadcasted_iota(I32, (c, 2 * c), 0)
    c2 = lax.broadcasted_iota(I32, (c, 2 * c), 1) % c
    keep = (cc <= r, cc >= r)
    keep2 = (c2 <= r2, c2 >= r2)
    tri = tuple(jnp.where(m, 1.0, 0.0).astype(BF16) for m in keep)
    head0 = lax.broadcasted_iota(I32, (c, LANES), 1) < GLA_DK
    vhead0 = lax.broadcasted_iota(I32, (c, 2 * GLA_DV), 1) < GLA_DV
    same_head = ((lax.broadcasted_iota(I32, (2 * GLA_DV, LANES), 0) < GLA_DV)
                 == (lax.broadcasted_iota(I32, (2 * GLA_DV, LANES), 1) < GLA_DK))
    items = [(d, p, s) for s in range(nchunk) for p in range(npair) for d in range(2)]

    def where_(d, p, s):
        ci = s if d == 0 else nchunk - 1 - s
        return slice(ci * c, (ci + 1) * c), slice(p * LANES, (p + 1) * LANES), slice(p * 2 * GLA_DV, (p + 1) * 2 * GLA_DV)

    b3 = {}
    for it in items:
        d, p, s = it
        rows, kl, _ = where_(*it)
        b3[it] = _dot(tri[d], jnp.concatenate(_split3(refs[d][3][rows, kl]), axis=1))
    qt, decay, amat, kvm, vbd = {}, {}, {}, {}, {}
    for it in items:
        d, p, s = it
        rows, kl, vl = where_(*it)
        b = b3[it][:, :LANES] + b3[it][:, LANES:2 * LANES] + b3[it][:, 2 * LANES:]
        b_last = b[0:1, :] if d == 1 else b[c - 1:c, :]
        q = refs[d][0][rows, kl]
        k = refs[d][1][rows, kl]
        qt[it] = (q * jnp.exp(b)).astype(BF16)
        kt = (k * jnp.exp(-b)).astype(BF16)
        kd = (k * jnp.exp(b_last - b)).astype(BF16)
        decay[it] = jnp.exp(b_last)
        zk = jnp.zeros_like(kt)
        kstack = jnp.concatenate([jnp.where(head0, kt, zk), jnp.where(head0, zk, kt)], axis=0)
        a = lax.dot_general(qt[it], kstack, _NT, preferred_element_type=F32)
        amat[it] = jnp.where(keep2[d], a, 0.0).astype(BF16)
        v = refs[d][2][rows, vl].astype(BF16)
        zv = jnp.zeros_like(v)
        vbd[it] = jnp.concatenate([jnp.where(vhead0, v, zv), jnp.where(vhead0, zv, v)], axis=0)
        kv = lax.dot_general(v, kd, _TN, preferred_element_type=F32)
        kvm[it] = jnp.where(same_head, kv, 0.0)
    pre = {}
    for d in range(2):
        for p in range(npair):
            st = st_ref[d, p]
            for s in range(nchunk):
                it = (d, p, s)
                pre[it] = st.astype(BF16)
                st = st * decay[it] + kvm[it]
            st_ref[d, p] = st
    for it in items:
        d, p, s = it
        rows, _, vl = where_(*it)
        refs[d][4][rows, vl] = (_dot(amat[it], vbd[it])
                                + lax.dot_general(qt[it], pre[it], _NT, preferred_element_type=F32))


def _gla(gq, gk, gv, gf, gb, nbatch, seq, tc):
    nt = gq.shape[0]
    nj = seq // tc
    fwd = lambda b, j: (b * nj + j, 0)
    bwd = lambda b, j: (b * nj + nj - 1 - j, 0)
    kspec = lambda m: pl.BlockSpec((tc, GLA_KEY_WIDTH), m)
    vspec = lambda m: pl.BlockSpec((tc, GLA_VAL_WIDTH), m)
    return pl.pallas_call(
        functools.partial(_gla_kernel, tc=tc),
        grid=(nbatch, nj),
        in_specs=[kspec(fwd), kspec(fwd), vspec(fwd), kspec(fwd), kspec(bwd), kspec(bwd), vspec(bwd), kspec(bwd)],
        out_specs=[vspec(fwd), vspec(bwd)],
        out_shape=[jax.ShapeDtypeStruct((nt, GLA_VAL_WIDTH), F32)] * 2,
        scratch_shapes=[pltpu.VMEM((2, GLA_HEADS // 2, 2 * GLA_DV, LANES), F32)],
        compiler_params=_cparams("parallel", "arbitrary"),
        name="bi_gla",
    )(gq, gk, gv, gf, gq, gk, gv, gb)


def _merge_kernel(x_ref, ao_ref, ch_ref, of_ref, ob_ref, gr_ref, ng_ref, wa_ref, wc_ref, wl_ref, wmg_ref, wout_ref,
                  g1_ref, b1_ref, wr_ref, x1_ref, x1b_ref, aff_ref):
    x = x_ref[...]
    xb = x.astype(BF16)
    a = _dot(ao_ref[...], wa_ref[...])
    c = _dot(ch_ref[...], wc_ref[...])
    o = of_ref[...] + ob_ref[...]
    r = gr_ref[...]
    ng = ng_ref[...]
    parts = []
    for h in range(GLA_HEADS):
        oh = o[:, h * GLA_DV:(h + 1) * GLA_DV]
        rh = r[:, h * GLA_DV:(h + 1) * GLA_DV]
        yh = oh * lax.rsqrt(jnp.mean(oh * oh, axis=1, keepdims=True) + NORM_EPS) * ng
        parts.append((yh * (rh * _sigmoid(rh))).astype(BF16))
    l = _dot(jnp.concatenate(parts, axis=1), wl_ref[...])
    m = _sigmoid(_dot(xb, wmg_ref[:, 0:D_MODEL])) * a
    m = m + _sigmoid(_dot(xb, wmg_ref[:, D_MODEL:2 * D_MODEL])) * c
    m = m + _sigmoid(_dot(xb, wmg_ref[:, 2 * D_MODEL:3 * D_MODEL])) * l
    y = DEEPNORM_ALPHA * x + _dot(m.astype(BF16), wout_ref[...])
    x1 = _layernorm(y, g1_ref[...], b1_ref[...])
    x1_ref[...] = x1
    x1b = x1.astype(BF16)
    x1b_ref[...] = x1b
    logits = lax.dot_general(wr_ref[...], x1b, _NT, preferred_element_type=F32)
    ex = jnp.exp(logits - jnp.max(logits, axis=0, keepdims=True))
    aff_ref[...] = ex / jnp.sum(ex, axis=0, keepdims=True)


def _merge(x, ao, ch, of, ob, gr, w, tm):
    nt = x.shape[0]
    row = lambda i: (i, 0)
    const = lambda i: (0, 0)
    full = lambda a: pl.BlockSpec(a.shape, const)
    names = ["ng", "wa", "wc", "wl", "wmg", "wout", "g1", "b1", "wr"]
    return pl.pallas_call(
        _merge_kernel,
        grid=(nt // tm,),
        in_specs=[pl.BlockSpec((tm, D_MODEL), row), pl.BlockSpec((tm, DA_WIDTH), row),
                  pl.BlockSpec((tm, CONV_WIDTH), row), pl.BlockSpec((tm, GLA_VAL_WIDTH), row),
                  pl.BlockSpec((tm, GLA_VAL_WIDTH), row), pl.BlockSpec((tm, GLA_VAL_WIDTH), row)]
                 + [full(w[n]) for n in names],
        out_specs=[pl.BlockSpec((tm, D_MODEL), row), pl.BlockSpec((tm, D_MODEL), row),
                   pl.BlockSpec((N_EXPERTS, tm), lambda i: (0, i))],
        out_shape=[jax.ShapeDtypeStruct((nt, D_MODEL), F32), jax.ShapeDtypeStruct((nt, D_MODEL), BF16),
                   jax.ShapeDtypeStruct((N_EXPERTS, nt), F32)],
        compiler_params=_cparams("parallel"),
        name="merge_ln1_router",
    )(x, ao, ch, of, ob, gr, *[w[n] for n in names])


def _select_kernel(aff_ref, pos_ref, offs_ref, sel_ref, *, cap, slot_base, idx_bits):
    ne, n = aff_ref.shape
    bits = pltpu.bitcast(aff_ref[...], I32)
    capf = float(cap)

    def count(mask):
        return jnp.sum(jnp.where(mask, 1.0, 0.0), axis=1, keepdims=True)

    def value_step(i, t):
        cand = t | jnp.left_shift(jnp.int32(1), 30 - i)
        return jnp.where(count(bits >= cand) >= capf, cand, t)

    thr = lax.fori_loop(0, 31, value_step, jnp.zeros((ne, 1), I32))
    above = bits > thr
    tie = bits == thr
    need = capf - count(above)
    idx = lax.broadcasted_iota(I32, (ne, n), 1)

    def index_step(i, ans):
        cand = ans | jnp.left_shift(jnp.int32(1), idx_bits - 1 - i)
        return jnp.where(count(tie & (idx < cand)) < need, cand, ans)

    last_tie = lax.fori_loop(0, idx_bits, index_step, jnp.zeros((ne, 1), I32))
    sel_ref[...] = jnp.where(above | (tie & (idx <= last_tie)), 1.0, 0.0)

    ch = TOKEN_CHUNK
    upper = jnp.where(lax.broadcasted_iota(I32, (ch, ch), 0) <= lax.broadcasted_iota(I32, (ch, ch), 1),
                      1.0, 0.0).astype(BF16)
    lane = lax.broadcasted_iota(I32, (ne, LANES), 1)

    offs_ref[...] = jnp.zeros_like(offs_ref)

    def chunk_step(j, run):
        off = pl.multiple_of(j * ch, ch)
        m = sel_ref[:, pl.ds(off, ch)]
        incl = _dot(m.astype(BF16), upper)
        p = jnp.where(m > 0.0, run + incl + (slot_base - 1.0), -1.0)
        pos_ref[:, pl.ds(off, ch)] = p.astype(I32)
        offs_ref[...] = jnp.where(lane == j, jnp.broadcast_to(run, (ne, LANES)).astype(I32), offs_ref[...])
        return run + incl[:, ch - 1:ch]

    lax.fori_loop(0, n // ch, chunk_step, jnp.zeros((ne, 1), F32))


def _select(aff, cap, slot_base):
    ne, n = aff.shape
    assert n % TOKEN_CHUNK == 0 and n // TOKEN_CHUNK <= LANES
    return pl.pallas_call(
        functools.partial(_select_kernel, cap=cap, slot_base=slot_base, idx_bits=max(1, (n - 1).bit_length())),
        out_shape=[jax.ShapeDtypeStruct((ne, n), I32), jax.ShapeDtypeStruct((ne, LANES), I32)],
        scratch_shapes=[pltpu.VMEM((ne, n), F32)],
        compiler_params=pltpu.CompilerParams(vmem_limit_bytes=VMEM_LIMIT),
        name="expert_select",
    )(aff)


GATHER_CHUNKS = 10
GATE_ROWS = 16


def _ffn_kernel(blo_ref, bhi_ref, pos_ref, aff_ref, x_hbm, wg_ref, wu_ref, wd_ref, ye_ref, xcat, sem, pcat, gcat,
                xe_ref, gate_ref, *, ts, sub):
    nj = pl.num_programs(1)
    j = pl.program_id(1)
    step = pl.program_id(0) * nj + j
    nsub = ts // sub
    nunits = pl.num_programs(0) * nj * nsub
    ch = TOKEN_CHUNK
    kb = GATHER_CHUNKS
    last_chunk = x_hbm.shape[0] // ch - 1

    def copy(c, which, k):
        return pltpu.make_async_copy(x_hbm.at[pl.ds(pl.multiple_of(c * ch, ch), ch), :],
                                     xcat.at[which, pl.ds(k * ch, ch), :], sem.at[which])

    def issue(lo, cnt, which):
        for k in range(kb):
            @pl.when(k < cnt)
            def _(k=k):
                copy(lo + k, which, k).start()

    def drain(cnt, which):
        for k in range(kb):
            @pl.when(k < cnt)
            def _(k=k):
                copy(0, which, k).wait()

    def unit_bounds(u):
        lo = blo_ref[u]
        return lo, bhi_ref[u] - lo

    @pl.when(step == 0)
    def _():
        xcat[...] = jnp.zeros_like(xcat)
        lo, n = unit_bounds(0)
        issue(lo, jnp.minimum(n, kb), 0)

    piece = lax.broadcasted_iota(I32, (GATE_ROWS, ch), 0)
    for t in range(nsub):
        u = step * nsub + t
        which = u % 2
        lo, n = unit_bounds(u)
        slot_id = lax.broadcasted_iota(I32, (sub, ch), 0) + (j * ts + t * sub)

        @pl.when(u + 1 < nunits)
        def _(u=u, which=which):
            lo1, n1 = unit_bounds(u + 1)
            issue(lo1, jnp.minimum(n1, kb), 1 - which)

        def build(first, cnt):
            for k in range(kb):
                c = jnp.minimum(first + k, last_chunk)
                off = pl.multiple_of(c * ch, ch)
                hit = (pos_ref[:, pl.ds(off, ch)] == slot_id) & (k < cnt)
                pcat[:, k * ch:(k + 1) * ch] = jnp.where(hit, 1.0, 0.0).astype(BF16)
                hi, mid, lo3 = _split3(aff_ref[:, pl.ds(off, ch)])
                gcat[:, k * ch:(k + 1) * ch] = jnp.where(
                    piece == 0, hi.astype(F32), jnp.where(piece == 1, mid.astype(F32),
                                                          jnp.where(piece == 2, lo3.astype(F32), 0.0))).astype(BF16)

        cnt0 = jnp.minimum(n, kb)
        build(lo, cnt0)
        drain(cnt0, which)
        rows = slice(t * sub, (t + 1) * sub)
        xe_ref[rows, :] = _dot(pcat[...], xcat[which])
        gate_ref[rows, :] = lax.dot_general(pcat[...], gcat[...], _NT, preferred_element_type=F32)

        def more(b, carry, which=which, lo=lo, n=n, rows=rows):
            first = lo + b * kb
            cnt = jnp.minimum(n - b * kb, kb)
            issue(first, cnt, which)
            build(first, cnt)
            drain(cnt, which)
            xe_ref[rows, :] += _dot(pcat[...], xcat[which])
            gate_ref[rows, :] += lax.dot_general(pcat[...], gcat[...], _NT, preferred_element_type=F32)
            return carry

        lax.fori_loop(1, (n + kb - 1) // kb, more, 0)

    xe = xe_ref[...].astype(BF16)
    g = _dot(xe, wg_ref[...])
    uu = _dot(xe, wu_ref[...])
    h = (g * _sigmoid(g)) * uu
    gate = jnp.sum(gate_ref[...], axis=1, keepdims=True)
    ye_ref[...] = (_dot(h.astype(BF16), wd_ref[...]) * gate).astype(BF16)


def _ffn(blo, bhi, pos3, aff3, x1b, wg, wu, wd, layer, cap_total, ts, sub):
    ntile = cap_total // ts
    nt = x1b.shape[0]
    per_expert = lambda e, j, *_: (e, 0, 0)
    layer_expert = lambda e, j, *_: (layer, e, 0, 0)
    return pl.pallas_call(
        functools.partial(_ffn_kernel, ts=ts, sub=sub),
        grid_spec=pltpu.PrefetchScalarGridSpec(
            num_scalar_prefetch=2,
            grid=(N_EXPERTS, ntile),
            in_specs=[pl.BlockSpec((None, 1, nt), per_expert),
                      pl.BlockSpec((None, 1, nt), per_expert),
                      pl.BlockSpec(memory_space=pl.ANY),
                      pl.BlockSpec((None, None, D_MODEL, EXPERT_FF), layer_expert),
                      pl.BlockSpec((None, None, D_MODEL, EXPERT_FF), layer_expert),
                      pl.BlockSpec((None, None, EXPERT_FF, D_MODEL), layer_expert)],
            out_specs=pl.BlockSpec((None, ts, D_MODEL), lambda e, j, *_: (e, j, 0)),
            scratch_shapes=[pltpu.VMEM((2, GATHER_CHUNKS * TOKEN_CHUNK, D_MODEL), BF16),
                            pltpu.SemaphoreType.DMA((2,)),
                            pltpu.VMEM((sub, GATHER_CHUNKS * TOKEN_CHUNK), BF16),
                            pltpu.VMEM((GATE_ROWS, GATHER_CHUNKS * TOKEN_CHUNK), BF16),
                            pltpu.VMEM((ts, D_MODEL), F32), pltpu.VMEM((ts, GATE_ROWS), F32)]),
        out_shape=jax.ShapeDtypeStruct((N_EXPERTS, cap_total, D_MODEL), BF16),
        compiler_params=_cparams("arbitrary", "arbitrary"),
        name="gather_expert_ffn",
    )(blo, bhi, pos3, aff3, x1b, wg, wu, wd)


def _combine_kernel(st_ref, nw_ref, x1_ref, post_ref, g2_ref, b2_ref, ye_hbm, *rest, cap_total, split):
    nout = 1 if split is None else 2
    out_refs, (ybuf, sem, xtra, sem_x, p_ref, acc_ref) = rest[:nout], rest[nout:]
    i = pl.program_id(0)
    ntiles = pl.num_programs(0)
    tm = x1_ref.shape[0]
    w = SLOT_WINDOW
    cur = i % 2

    def window(tile, e, k):
        lo = st_ref[tile * N_EXPERTS + e] + k * w
        return lo, pl.multiple_of(jnp.minimum(lo, cap_total - w), SLOT_ALIGN)

    def first_copy(tile, e, which):
        return pltpu.make_async_copy(ye_hbm.at[e, pl.ds(window(tile, e, 0)[1], w), :],
                                     ybuf.at[which, pl.ds(e * w, w), :], sem.at[which, e])

    @pl.when(i == 0)
    def _():
        for e in range(N_EXPERTS):
            first_copy(0, e, 0).start()

    @pl.when(i + 1 < ntiles)
    def _():
        for e in range(N_EXPERTS):
            first_copy(i + 1, e, 1 - cur).start()

    lane = lax.broadcasted_iota(I32, (tm, w), 1)

    def onehot(e, k):
        lo, start = window(i, e, k)
        pe = post_ref[:, e:e + 1]
        rel = jnp.where((pe >= lo) & (pe < start + w), pe - start, -1)
        return jnp.where(rel == lane, 1.0, 0.0).astype(BF16)

    for e in range(N_EXPERTS):
        p_ref[:, e * w:(e + 1) * w] = onehot(e, 0)
    for e in range(N_EXPERTS):
        first_copy(i, e, cur).wait()
    acc_ref[...] = _dot(p_ref[...], ybuf[cur])
    for e in range(N_EXPERTS):
        def extra(k, carry, e=e):
            cp = pltpu.make_async_copy(ye_hbm.at[e, pl.ds(window(i, e, k)[1], w), :], xtra, sem_x.at[0])
            cp.start()
            cp.wait()
            acc_ref[...] += _dot(onehot(e, k), xtra[...])
            return carry

        lax.fori_loop(1, nw_ref[i * N_EXPERTS + e], extra, 0)
    y = _layernorm(DEEPNORM_ALPHA * x1_ref[...] + acc_ref[...], g2_ref[...], b2_ref[...])
    if split is None:
        out_refs[0][...] = y
    else:
        @pl.when(i < split)
        def _():
            out_refs[0][...] = y

        @pl.when(i >= split)
        def _():
            out_refs[1][...] = y


def _combine(starts, nwin, x1, post, g2, b2, ye, tm, split=None):
    nt = x1.shape[0]
    ntiles = nt // tm
    cap_total = ye.shape[1]
    row = lambda i, *_: (i, 0)
    const = lambda i, *_: (0, 0)
    if split is None:
        out_specs = [pl.BlockSpec((tm, D_MODEL), row)]
        out_shape = [jax.ShapeDtypeStruct((nt, D_MODEL), F32)]
    else:
        out_specs = [pl.BlockSpec((tm, D_MODEL), lambda i, *_: (jnp.minimum(i, split - 1), 0)),
                     pl.BlockSpec((tm, D_MODEL), lambda i, *_: (jnp.maximum(i - split, 0), 0))]
        out_shape = [jax.ShapeDtypeStruct((split * tm, D_MODEL), F32),
                     jax.ShapeDtypeStruct(((ntiles - split) * tm, D_MODEL), F32)]
    return pl.pallas_call(
        functools.partial(_combine_kernel, cap_total=cap_total, split=split),
        grid_spec=pltpu.PrefetchScalarGridSpec(
            num_scalar_prefetch=2,
            grid=(ntiles,),
            in_specs=[pl.BlockSpec((tm, D_MODEL), row), pl.BlockSpec((tm, N_EXPERTS), row),
                      pl.BlockSpec(g2.shape, const), pl.BlockSpec(b2.shape, const),
                      pl.BlockSpec(memory_space=pl.ANY)],
            out_specs=out_specs,
            scratch_shapes=[pltpu.VMEM((2, N_EXPERTS * SLOT_WINDOW, D_MODEL), BF16),
                            pltpu.SemaphoreType.DMA((2, N_EXPERTS)),
                            pltpu.VMEM((SLOT_WINDOW, D_MODEL), BF16), pltpu.SemaphoreType.DMA((1,)),
                            pltpu.VMEM((tm, N_EXPERTS * SLOT_WINDOW), BF16), pltpu.VMEM((tm, D_MODEL), F32)]),
        out_shape=out_shape,
        compiler_params=_cparams("arbitrary"),
        name="combine_ln2",
    )(starts, nwin, x1, post, g2, b2, ye)


def _tile_bounds(offs_groups, caps, chunk_bases, ts):
    los, his = [], []
    for offs, cap, cbase in zip(offs_groups, caps, chunk_bases):
        nch = offs.shape[1]
        ends = jnp.concatenate([offs[:, 1:], jnp.full((N_EXPERTS, 1), cap, I32)], axis=1)
        s0 = (jnp.arange(cap // ts, dtype=I32) * ts)[None, :, None]
        los.append(cbase + jnp.sum((ends[:, None, :] <= s0).astype(I32), axis=2))
        his.append(cbase + jnp.sum((offs[:, None, :] < s0 + ts).astype(I32), axis=2))
        del nch
    return jnp.concatenate(los, axis=1).reshape(-1), jnp.concatenate(his, axis=1).reshape(-1)


def _window_bounds(offs_groups, caps, slot_bases, tm):
    starts, nwins = [], []
    per = tm // TOKEN_CHUNK
    for offs, cap, sbase in zip(offs_groups, caps, slot_bases):
        ends = jnp.concatenate([offs, jnp.full((N_EXPERTS, 1), cap, I32)], axis=1)
        first = ends[:, 0:-1:per] + sbase
        stop = ends[:, per::per] + sbase
        st = (first // SLOT_ALIGN) * SLOT_ALIGN
        nw = jnp.maximum((stop - st + SLOT_WINDOW - 1) // SLOT_WINDOW, 1)
        starts.append(st.T)
        nwins.append(nw.T)
    return jnp.concatenate(starts, axis=0).reshape(-1), jnp.concatenate(nwins, axis=0).reshape(-1)


def _prep_layer(l, p):
    bf = lambda a: a.astype(BF16)
    w_in = p["w_in"][l]
    c = 0
    cuts = {}
    for name, width in (("q", DA_WIDTH), ("k", DA_WIDTH), ("v", DA_WIDTH), ("cu", 2 * CONV_WIDTH),
                        ("g4", 2 * GLA_KEY_WIDTH + 2 * GLA_VAL_WIDTH), ("lr", 2 * GLA_GATE_RANK),
                        ("mg", N_BRANCH * D_MODEL)):
        cuts[name] = w_in[:, c:c + width]
        c += width
    gw2 = p["gla_gate_w2"][l]
    gw = jnp.zeros((LANES, 2 * GLA_KEY_WIDTH), F32)
    gw = gw.at[0:GLA_GATE_RANK, 0:GLA_KEY_WIDTH].set(gw2[0])
    gw = gw.at[GLA_GATE_RANK:2 * GLA_GATE_RANK, GLA_KEY_WIDTH:].set(gw2[1])
    lamp = jnp.zeros((8, LANES), F32)
    for r, nm in enumerate(("da_lam_q1", "da_lam_k1", "da_lam_q2", "da_lam_k2")):
        lamp = lamp.at[r, 0:DA_HEAD_DIM].set(p[nm][l])
    row = lambda a: a.reshape(1, -1)
    return {
        "wqk": bf(jnp.concatenate([cuts["q"], cuts["k"]], axis=1)), "wv": bf(cuts["v"]), "wcu": bf(cuts["cu"]),
        "wg4": bf(cuts["g4"]), "wlr": bf(jnp.pad(cuts["lr"], ((0, 0), (0, LANES - 2 * GLA_GATE_RANK)))),
        "gw": bf(gw), "gb": p["gla_gate_b"][l].reshape(1, -1),
        "lamp": lamp, "subln": row(p["da_subln_g"][l]),
        "conv_w": jnp.pad(p["conv_w"][l], ((0, 1), (0, 0))), "conv_b": row(p["conv_b"][l]),
        "conv_lg": row(p["conv_ln_g"][l]), "conv_lb": row(p["conv_ln_b"][l]),
        "ng": row(p["gla_norm_g"][l]), "wa": bf(p["da_w_o"][l]), "wc": bf(p["conv_w_o"][l]),
        "wl": bf(p["gla_w_o"][l]), "wmg": bf(cuts["mg"]), "wout": bf(p["w_out"][l]),
        "g1": row(p["ln1_g"][l]), "b1": row(p["ln1_b"][l]), "wr": bf(p["w_router"][l].T),
        "g2": row(p["ln2_g"][l]), "b2": row(p["ln2_b"][l]),
    }


def _rope_tables(seq):
    d = DA_HEAD_DIM
    inv = 1.0 / (ROPE_THETA ** (jnp.arange(0, d, 2, dtype=F32) / d))
    ang = jnp.arange(seq, dtype=F32)[:, None] * inv[None, :]
    c, s = jnp.cos(ang), jnp.sin(ang)
    cos = jnp.concatenate([c, c], axis=1)
    sin = jnp.concatenate([-s, s], axis=1)
    reps = LANES // d
    return jnp.tile(cos, (1, reps)), jnp.tile(sin, (1, reps))


def _tiles(seq, group_tokens, caps):
    g = functools.reduce(math.gcd, group_tokens)
    gc = functools.reduce(math.gcd, caps)
    return {
        "proj": min(512, seq), "attn": min(256, seq), "conv": min(256, seq), "gla": min(512, seq),
        "merge": min(512, seq), "ffn": min(512, gc), "gather": min(256, gc), "combine": min(512, g),
    }


def _encode(xs, p, depth):
    seq = xs[0].shape[1]
    nbatch = sum(x.shape[0] for x in xs)
    group_tokens = [x.shape[0] * seq for x in xs]
    caps = [CAPACITY_FACTOR * n // N_EXPERTS for n in group_tokens]
    slot_bases = [sum(caps[:i]) for i in range(len(caps))]
    token_bases = [sum(group_tokens[:i]) for i in range(len(caps))]
    chunk_bases = [t // TOKEN_CHUNK for t in token_bases]
    cap_total = sum(caps)
    t = _tiles(seq, group_tokens, caps)
    assert cap_total >= SLOT_WINDOW and all(c % t["ffn"] == 0 for c in caps)
    x = jnp.concatenate([x.reshape(-1, D_MODEL) for x in xs], axis=0)
    cos, sin = _rope_tables(seq)
    wgate, wup, wdown = (p[n].astype(BF16) for n in ("w_gate", "w_up", "w_down"))
    for l in range(depth):
        w = _prep_layer(l, p)
        lam_init = 0.8 - 0.6 * math.exp(-0.3 * l)
        q, k, v, u, gq, gk, gv, gr, gf, gb = _proj(x, cos, sin, w, seq, t["proj"])
        ao = _attn(q, k, v, w["lamp"], w["subln"], lam_init, nbatch, seq, t["attn"])
        ch = _conv(u, w["conv_w"], w["conv_b"], w["conv_lg"], w["conv_lb"], nbatch, seq, t["conv"])
        of, ob = _gla(gq, gk, gv, gf, gb, nbatch, seq, t["gla"])
        x1, x1b, aff = _merge(x, ao, ch, of, ob, gr, w, t["merge"])
        pos_g, offs_g = [], []
        for n, tb, cap, sb in zip(group_tokens, token_bases, caps, slot_bases):
            pos, offs = _select(aff[:, tb:tb + n], cap, sb)
            pos_g.append(pos)
            offs_g.append(offs[:, :n // TOKEN_CHUNK])
        pos = jnp.concatenate(pos_g, axis=1)
        blo, bhi = _tile_bounds(offs_g, caps, chunk_bases, t["gather"])
        ye = _ffn(blo, bhi, pos.reshape(N_EXPERTS, 1, -1), aff.reshape(N_EXPERTS, 1, -1), x1b,
                  wgate, wup, wdown, l, cap_total, t["ffn"], t["gather"])
        starts, nwin = _window_bounds(offs_g, caps, slot_bases, t["combine"])
        split = group_tokens[0] // t["combine"] if l == depth - 1 else None
        res = _combine(starts, nwin, x1, pos.T, w["g2"], w["b2"], ye, t["combine"], split=split)
        x = res[0]
    return tuple(xo.reshape(xg.shape) for xo, xg in zip(res, xs))


def kernel(x_prompt, x_sample, w_in, da_lam_q1, da_lam_k1, da_lam_q2, da_lam_k2, da_subln_g, da_w_o, conv_w, conv_b, conv_ln_g, conv_ln_b, conv_w_o, gla_gate_w2, gla_gate_b, gla_norm_g, gla_w_o, w_out, ln1_g, ln1_b, w_router, w_gate, w_up, w_down, ln2_g, ln2_b):
    p = dict(w_in=w_in, da_lam_q1=da_lam_q1, da_lam_k1=da_lam_k1, da_lam_q2=da_lam_q2, da_lam_k2=da_lam_k2,
             da_subln_g=da_subln_g, da_w_o=da_w_o, conv_w=conv_w, conv_b=conv_b, conv_ln_g=conv_ln_g,
             conv_ln_b=conv_ln_b, conv_w_o=conv_w_o, gla_gate_w2=gla_gate_w2, gla_gate_b=gla_gate_b,
             gla_norm_g=gla_norm_g, gla_w_o=gla_w_o, w_out=w_out, ln1_g=ln1_g, ln1_b=ln1_b, w_router=w_router,
             w_gate=w_gate, w_up=w_up, w_down=w_down, ln2_g=ln2_g, ln2_b=ln2_b)
    return _encode([x_prompt, x_sample], p, DEPTH)
```

```python
import functools
import math

import jax
import jax.numpy as jnp
from jax import lax
from jax.experimental import pallas as pl
from jax.experimental.pallas import tpu as pltpu

F32 = jnp.float32
BF16 = jnp.bfloat16
I32 = jnp.int32

D_MODEL = 1024
DEPTH = 4
DA_HEADS = 4
DA_HEAD_DIM = 64
DA_WIDTH = DA_HEADS * 2 * DA_HEAD_DIM
ROPE_THETA = 10000.0
CONV_WIDTH = 512
CONV_KERNEL = 31
GLA_HEADS = 4
GLA_DK = 64
GLA_DV = 128
GLA_KEY_WIDTH = GLA_HEADS * GLA_DK
GLA_VAL_WIDTH = GLA_HEADS * GLA_DV
GLA_GATE_RANK = 16
GLA_TAU = 16.0
GLA_CHUNK = 64
N_BRANCH = 3
N_EXPERTS = 16
EXPERT_FF = 2048
CAPACITY_FACTOR = 2
DEEPNORM_ALPHA = (2 * DEPTH) ** 0.25
NORM_EPS = 1e-5
LOG2E = 1.4426950408889634

LANES = 128
TOKEN_CHUNK = 256
SLOT_WINDOW = 128
SLOT_ALIGN = 16
CONV_HALO = 16
VMEM_LIMIT = 56 * 1024 * 1024

_NT = (((1,), (1,)), ((), ()))
_TN = (((0,), (0,)), ((), ()))


def _cparams(*sem):
    return pltpu.CompilerParams(dimension_semantics=sem, vmem_limit_bytes=VMEM_LIMIT)


def _dot(a, b):
    return jnp.dot(a, b, preferred_element_type=F32)


def _sigmoid(x):
    return 1.0 / (1.0 + jnp.exp(-x))


def _layernorm(y, g, b):
    mu = jnp.mean(y, axis=-1, keepdims=True)
    yc = y - mu
    var = jnp.mean(yc * yc, axis=-1, keepdims=True)
    return yc * lax.rsqrt(var + NORM_EPS) * g + b


def _proj_kernel(x_ref, cos_ref, sin_ref, wqk_ref, wv_ref, wcu_ref, wg4_ref, wlr_ref, gw_ref, gb_ref,
                 q_ref, k_ref, v_ref, u_ref, gq_ref, gk_ref, gv_ref, gr_ref, gf_ref, gbw_ref):
    xb = x_ref[...].astype(BF16)
    tm = xb.shape[0]
    qk = _dot(xb, wqk_ref[...])
    cos = cos_ref[...]
    sin = sin_ref[...]
    lane = lax.broadcasted_iota(I32, (tm, LANES), 1)
    first_half = (lane % DA_HEAD_DIM) < (DA_HEAD_DIM // 2)
    nqb = DA_WIDTH // LANES
    for cb in range(2 * nqb):
        xc = qk[:, cb * LANES:(cb + 1) * LANES]
        rot = jnp.where(first_half, pltpu.roll(xc, LANES - DA_HEAD_DIM // 2, 1), pltpu.roll(xc, DA_HEAD_DIM // 2, 1))
        r = xc * cos + rot * sin
        if cb < nqb:
            q_ref[:, cb * LANES:(cb + 1) * LANES] = (r * (DA_HEAD_DIM ** -0.5 * LOG2E)).astype(BF16)
        else:
            k_ref[:, (cb - nqb) * LANES:(cb - nqb + 1) * LANES] = r.astype(BF16)
    v_ref[...] = _dot(xb, wv_ref[...]).astype(BF16)
    cu = _dot(xb, wcu_ref[...])
    u_ref[...] = cu[:, :CONV_WIDTH] * _sigmoid(cu[:, CONV_WIDTH:])
    g4 = _dot(xb, wg4_ref[...])
    kw, vw = GLA_KEY_WIDTH, GLA_VAL_WIDTH
    gq_ref[...] = g4[:, :kw] * (GLA_DK ** -0.5)
    gk_ref[...] = g4[:, kw:2 * kw]
    gv_ref[...] = g4[:, 2 * kw:2 * kw + vw]
    gr_ref[...] = g4[:, 2 * kw + vw:]
    lr = _dot(xb, wlr_ref[...])
    gp = _dot(lr.astype(BF16), gw_ref[...]) + gb_ref[...]
    lsig = jnp.minimum(gp, 0.0) - jnp.log(1.0 + jnp.exp(-jnp.abs(gp)))
    lsig = lsig * (1.0 / GLA_TAU)
    gf_ref[...] = lsig[:, :kw]
    gbw_ref[...] = lsig[:, kw:]


def _proj(x, cos, sin, w, seq, tm):
    nt = x.shape[0]
    nseq = seq // tm
    row = lambda i: (i, 0)
    const = lambda i: (0, 0)
    full = lambda a: pl.BlockSpec(a.shape, const)
    outs = [(DA_WIDTH, BF16), (DA_WIDTH, BF16), (DA_WIDTH, BF16), (CONV_WIDTH, F32),
            (GLA_KEY_WIDTH, F32), (GLA_KEY_WIDTH, F32), (GLA_VAL_WIDTH, F32), (GLA_VAL_WIDTH, F32),
            (GLA_KEY_WIDTH, F32), (GLA_KEY_WIDTH, F32)]
    return pl.pallas_call(
        _proj_kernel,
        grid=(nt // tm,),
        in_specs=[pl.BlockSpec((tm, D_MODEL), row),
                  pl.BlockSpec((tm, LANES), lambda i: (i % nseq, 0)),
                  pl.BlockSpec((tm, LANES), lambda i: (i % nseq, 0)),
                  full(w["wqk"]), full(w["wv"]), full(w["wcu"]), full(w["wg4"]), full(w["wlr"]),
                  full(w["gw"]), full(w["gb"])],
        out_specs=[pl.BlockSpec((tm, c), row) for c, _ in outs],
        out_shape=[jax.ShapeDtypeStruct((nt, c), d) for c, d in outs],
        compiler_params=_cparams("parallel"),
        name="proj",
    )(x, cos, sin, w["wqk"], w["wv"], w["wcu"], w["wg4"], w["wlr"], w["gw"], w["gb"])


def _attn_kernel(lamp_ref, g_ref, q_ref, k_ref, v_ref, o_ref, *, lam_init):
    lp = lamp_ref[...]
    lam = (jnp.exp(jnp.sum(lp[0:1] * lp[1:2], axis=1, keepdims=True))
           - jnp.exp(jnp.sum(lp[2:3] * lp[3:4], axis=1, keepdims=True)) + lam_init)
    for h in range(DA_HEADS):
        cols = slice(h * LANES, (h + 1) * LANES)
        q = q_ref[:, cols]
        k = k_ref[:, cols]
        lane = lax.broadcasted_iota(I32, q.shape, 1)
        zero = jnp.zeros_like(q)
        s0 = lax.dot_general(jnp.where(lane < DA_HEAD_DIM, q, zero), k, _NT, preferred_element_type=F32)
        s1 = lax.dot_general(jnp.where(lane >= DA_HEAD_DIM, q, zero), k, _NT, preferred_element_type=F32)
        e0 = jnp.exp2(s0 - jnp.max(s0, axis=1, keepdims=True))
        e1 = jnp.exp2(s1 - jnp.max(s1, axis=1, keepdims=True))
        r0 = 1.0 / jnp.sum(e0, axis=1, keepdims=True)
        r1 = lam / jnp.sum(e1, axis=1, keepdims=True)
        p = e0 * r0 - e1 * r1
        o = _dot(p.astype(BF16), v_ref[:, cols])
        ms = jnp.mean(o * o, axis=1, keepdims=True)
        y = o * lax.rsqrt(ms + NORM_EPS) * g_ref[...] * (1.0 - lam_init)
        o_ref[:, cols] = y.astype(BF16)


def _attn(q, k, v, lamp, subln_g, lam_init, nbatch, seq, tq):
    nt = q.shape[0]
    nq = seq // tq
    return pl.pallas_call(
        functools.partial(_attn_kernel, lam_init=lam_init),
        grid=(nbatch, nq),
        in_specs=[pl.BlockSpec(lamp.shape, lambda b, i: (0, 0)),
                  pl.BlockSpec(subln_g.shape, lambda b, i: (0, 0)),
                  pl.BlockSpec((tq, DA_WIDTH), lambda b, i: (b * nq + i, 0)),
                  pl.BlockSpec((seq, DA_WIDTH), lambda b, i: (b, 0)),
                  pl.BlockSpec((seq, DA_WIDTH), lambda b, i: (b, 0))],
        out_specs=pl.BlockSpec((tq, DA_WIDTH), lambda b, i: (b * nq + i, 0)),
        out_shape=jax.ShapeDtypeStruct((nt, DA_WIDTH), BF16),
        compiler_params=_cparams("parallel", "parallel"),
        name="diff_attn",
    )(lamp, subln_g, q, k, v)


CONV_ROWS = 32


SUBLANES = 8


def _conv_kernel(prev_ref, cur_ref, next_ref, w_ref, cb_ref, lg_ref, lb_ref, o_ref, scr, shifted, *, ts, nblk):
    i = pl.program_id(1)
    scr[0:CONV_HALO, :] = jnp.where(i > 0, prev_ref[...], 0.0)
    scr[CONV_HALO:CONV_HALO + ts, :] = cur_ref[...]
    scr[CONV_HALO + ts:2 * CONV_HALO + ts, :] = jnp.where(i < nblk - 1, next_ref[...], 0.0)
    span = ts + 2 * CONV_HALO - SUBLANES
    for s in range(1, SUBLANES):
        shifted[s - 1] = scr[s:s + span, :]
    w = w_ref[...]
    base = CONV_HALO - CONV_KERNEL // 2
    for rb in range(ts // CONV_ROWS):
        r0 = rb * CONV_ROWS
        acc = jnp.zeros((CONV_ROWS, CONV_WIDTH), F32)
        for t in range(CONV_KERNEL):
            phase = (base + t) % SUBLANES
            row = r0 + base + t - phase
            if phase == 0:
                xs = scr[row:row + CONV_ROWS, :]
            else:
                xs = shifted[phase - 1, row:row + CONV_ROWS, :]
            acc = acc + w[t:t + 1, :] * xs
        h = _layernorm(acc + cb_ref[...], lg_ref[...], lb_ref[...])
        o_ref[r0:r0 + CONV_ROWS, :] = (h * _sigmoid(h)).astype(BF16)


def _conv(u, w, cb, lg, lb, nbatch, seq, ts):
    nt = u.shape[0]
    nblk = seq // ts
    hb = ts // CONV_HALO
    nhalo = nt // CONV_HALO
    const = lambda b, i: (0, 0)
    return pl.pallas_call(
        functools.partial(_conv_kernel, ts=ts, nblk=nblk),
        grid=(nbatch, nblk),
        in_specs=[pl.BlockSpec((CONV_HALO, CONV_WIDTH), lambda b, i: (jnp.maximum((b * nblk + i) * hb - 1, 0), 0)),
                  pl.BlockSpec((ts, CONV_WIDTH), lambda b, i: (b * nblk + i, 0)),
                  pl.BlockSpec((CONV_HALO, CONV_WIDTH),
                               lambda b, i: (jnp.minimum((b * nblk + i + 1) * hb, nhalo - 1), 0)),
                  pl.BlockSpec(w.shape, const), pl.BlockSpec(cb.shape, const),
                  pl.BlockSpec(lg.shape, const), pl.BlockSpec(lb.shape, const)],
        out_specs=pl.BlockSpec((ts, CONV_WIDTH), lambda b, i: (b * nblk + i, 0)),
        out_shape=jax.ShapeDtypeStruct((nt, CONV_WIDTH), BF16),
        scratch_shapes=[pltpu.VMEM((ts + 2 * CONV_HALO, CONV_WIDTH), F32),
                        pltpu.VMEM((SUBLANES - 1, ts + 2 * CONV_HALO - SUBLANES, CONV_WIDTH), F32)],
        compiler_params=_cparams("parallel", "parallel"),
        name="conformer_conv",
    )(u, u, u, w, cb, lg, lb)


def _split3(g):
    hi = g.astype(BF16)
    r1 = g - hi.astype(F32)
    mid = r1.astype(BF16)
    lo = (r1 - mid.astype(F32)).astype(BF16)
    return hi, mid, lo


def _gla_kernel(qf_ref, kf_ref, vf_ref, gf_ref, qb_ref, kb_ref, vb_ref, gb_ref, of_ref, ob_ref, st_ref, *, tc):
    @pl.when(pl.program_id(1) == 0)
    def _():
        st_ref[...] = jnp.zeros_like(st_ref)

    c = GLA_CHUNK
    nchunk = tc // c
    npair = GLA_HEADS // 2
    refs = ((qf_ref, kf_ref, vf_ref, gf_ref, of_ref), (qb_ref, kb_ref, vb_ref, gb_ref, ob_ref))
    r = lax.broadcasted_iota(I32, (c, c), 0)
    cc = lax.broadcasted_iota(I32, (c, c), 1)
    r2 = lax.broadcasted_iota(I32, (c, 2 * c), 0)
    c2 = lax.broadcasted_iota(I32, (c, 2 * c), 1) % c
    keep = (cc <= r, cc >= r)
    keep2 = (c2 <= r2, c2 >= r2)
    tri = tuple(jnp.where(m, 1.0, 0.0).astype(BF16) for m in keep)
    head0 = lax.broadcasted_iota(I32, (c, LANES), 1) < GLA_DK
    vhead0 = lax.broadcasted_iota(I32, (c, 2 * GLA_DV), 1) < GLA_DV
    same_head = ((lax.broadcasted_iota(I32, (2 * GLA_DV, LANES), 0) < GLA_DV)
                 == (lax.broadcasted_iota(I32, (2 * GLA_DV, LANES), 1) < GLA_DK))
    items = [(d, p, s) for s in range(nchunk) for p in range(npair) for d in range(2)]

    def where_(d, p, s):
        ci = s if d == 0 else nchunk - 1 - s
        return slice(ci * c, (ci + 1) * c), slice(p * LANES, (p + 1) * LANES), slice(p * 2 * GLA_DV, (p + 1) * 2 * GLA_DV)

    b3 = {}
    for it in items:
        d, p, s = it
        rows, kl, _ = where_(*it)
        b3[it] = _dot(tri[d], jnp.concatenate(_split3(refs[d][3][rows, kl]), axis=1))
    qt, decay, amat, kvm, vbd = {}, {}, {}, {}, {}
    for it in items:
        d, p, s = it
        rows, kl, vl = where_(*it)
        b = b3[it][:, :LANES] + b3[it][:, LANES:2 * LANES] + b3[it][:, 2 * LANES:]
        b_last = b[0:1, :] if d == 1 else b[c - 1:c, :]
        q = refs[d][0][rows, kl]
        k = refs[d][1][rows, kl]
        qt[it] = (q * jnp.exp(b)).astype(BF16)
        kt = (k * jnp.exp(-b)).astype(BF16)
        kd = (k * jnp.exp(b_last - b)).astype(BF16)
        decay[it] = jnp.exp(b_last)
        zk = jnp.zeros_like(kt)
        kstack = jnp.concatenate([jnp.where(head0, kt, zk), jnp.where(head0, zk, kt)], axis=0)
        a = lax.dot_general(qt[it], kstack, _NT, preferred_element_type=F32)
        amat[it] = jnp.where(keep2[d], a, 0.0).astype(BF16)
        v = refs[d][2][rows, vl].astype(BF16)
        zv = jnp.zeros_like(v)
        vbd[it] = jnp.concatenate([jnp.where(vhead0, v, zv), jnp.where(vhead0, zv, v)], axis=0)
        kv = lax.dot_general(v, kd, _TN, preferred_element_type=F32)
        kvm[it] = jnp.where(same_head, kv, 0.0)
    pre = {}
    for d in range(2):
        for p in range(npair):
            st = st_ref[d, p]
            for s in range(nchunk):
                it = (d, p, s)
                pre[it] = st.astype(BF16)
                st = st * decay[it] + kvm[it]
            st_ref[d, p] = st
    for it in items:
        d, p, s = it
        rows, _, vl = where_(*it)
        refs[d][4][rows, vl] = (_dot(amat[it], vbd[it])
                                + lax.dot_general(qt[it], pre[it], _NT, preferred_element_type=F32))


def _gla(gq, gk, gv, gf, gb, nbatch, seq, tc):
    nt = gq.shape[0]
    nj = seq // tc
    fwd = lambda b, j: (b * nj + j, 0)
    bwd = lambda b, j: (b * nj + nj - 1 - j, 0)
    kspec = lambda m: pl.BlockSpec((tc, GLA_KEY_WIDTH), m)
    vspec = lambda m: pl.BlockSpec((tc, GLA_VAL_WIDTH), m)
    return pl.pallas_call(
        functools.partial(_gla_kernel, tc=tc),
        grid=(nbatch, nj),
        in_specs=[kspec(fwd), kspec(fwd), vspec(fwd), kspec(fwd), kspec(bwd), kspec(bwd), vspec(bwd), kspec(bwd)],
        out_specs=[vspec(fwd), vspec(bwd)],
        out_shape=[jax.ShapeDtypeStruct((nt, GLA_VAL_WIDTH), F32)] * 2,
        scratch_shapes=[pltpu.VMEM((2, GLA_HEADS // 2, 2 * GLA_DV, LANES), F32)],
        compiler_params=_cparams("parallel", "arbitrary"),
        name="bi_gla",
    )(gq, gk, gv, gf, gq, gk, gv, gb)


def _merge_kernel(x_ref, ao_ref, ch_ref, of_ref, ob_ref, gr_ref, ng_ref, wa_ref, wc_ref, wl_ref, wmg_ref, wout_ref,
                  g1_ref, b1_ref, wr_ref, x1_ref, x1b_ref, aff_ref):
    x = x_ref[...]
    xb = x.astype(BF16)
    a = _dot(ao_ref[...], wa_ref[...])
    c = _dot(ch_ref[...], wc_ref[...])
    o = of_ref[...] + ob_ref[...]
    r = gr_ref[...]
    ng = ng_ref[...]
    parts = []
    for h in range(GLA_HEADS):
        oh = o[:, h * GLA_DV:(h + 1) * GLA_DV]
        rh = r[:, h * GLA_DV:(h + 1) * GLA_DV]
        yh = oh * lax.rsqrt(jnp.mean(oh * oh, axis=1, keepdims=True) + NORM_EPS) * ng
        parts.append((yh * (rh * _sigmoid(rh))).astype(BF16))
    l = _dot(jnp.concatenate(parts, axis=1), wl_ref[...])
    m = _sigmoid(_dot(xb, wmg_ref[:, 0:D_MODEL])) * a
    m = m + _sigmoid(_dot(xb, wmg_ref[:, D_MODEL:2 * D_MODEL])) * c
    m = m + _sigmoid(_dot(xb, wmg_ref[:, 2 * D_MODEL:3 * D_MODEL])) * l
    y = DEEPNORM_ALPHA * x + _dot(m.astype(BF16), wout_ref[...])
    x1 = _layernorm(y, g1_ref[...], b1_ref[...])
    x1_ref[...] = x1
    x1b = x1.astype(BF16)
    x1b_ref[...] = x1b
    logits = lax.dot_general(wr_ref[...], x1b, _NT, preferred_element_type=F32)
    ex = jnp.exp(logits - jnp.max(logits, axis=0, keepdims=True))
    aff_ref[...] = ex / jnp.sum(ex, axis=0, keepdims=True)


def _merge(x, ao, ch, of, ob, gr, w, tm):
    nt = x.shape[0]
    row = lambda i: (i, 0)
    const = lambda i: (0, 0)
    full = lambda a: pl.BlockSpec(a.shape, const)
    names = ["ng", "wa", "wc", "wl", "wmg", "wout", "g1", "b1", "wr"]
    return pl.pallas_call(
        _merge_kernel,
        grid=(nt // tm,),
        in_specs=[pl.BlockSpec((tm, D_MODEL), row), pl.BlockSpec((tm, DA_WIDTH), row),
                  pl.BlockSpec((tm, CONV_WIDTH), row), pl.BlockSpec((tm, GLA_VAL_WIDTH), row),
                  pl.BlockSpec((tm, GLA_VAL_WIDTH), row), pl.BlockSpec((tm, GLA_VAL_WIDTH), row)]
                 + [full(w[n]) for n in names],
        out_specs=[pl.BlockSpec((tm, D_MODEL), row), pl.BlockSpec((tm, D_MODEL), row),
                   pl.BlockSpec((N_EXPERTS, tm), lambda i: (0, i))],
        out_shape=[jax.ShapeDtypeStruct((nt, D_MODEL), F32), jax.ShapeDtypeStruct((nt, D_MODEL), BF16),
                   jax.ShapeDtypeStruct((N_EXPERTS, nt), F32)],
        compiler_params=_cparams("parallel"),
        name="merge_ln1_router",
    )(x, ao, ch, of, ob, gr, *[w[n] for n in names])


def _select_kernel(aff_ref, pos_ref, offs_ref, sel_ref, *, cap, slot_base, idx_bits):
    ne, n = aff_ref.shape
    bits = pltpu.bitcast(aff_ref[...], I32)
    capf = float(cap)

    def count(mask):
        return jnp.sum(jnp.where(mask, 1.0, 0.0), axis=1, keepdims=True)

    def value_step(i, t):
        cand = t | jnp.left_shift(jnp.int32(1), 30 - i)
        return jnp.where(count(bits >= cand) >= capf, cand, t)

    thr = lax.fori_loop(0, 31, value_step, jnp.zeros((ne, 1), I32))
    above = bits > thr
    tie = bits == thr
    need = capf - count(above)
    idx = lax.broadcasted_iota(I32, (ne, n), 1)

    def index_step(i, ans):
        cand = ans | jnp.left_shift(jnp.int32(1), idx_bits - 1 - i)
        return jnp.where(count(tie & (idx < cand)) < need, cand, ans)

    last_tie = lax.fori_loop(0, idx_bits, index_step, jnp.zeros((ne, 1), I32))
    sel_ref[...] = jnp.where(above | (tie & (idx <= last_tie)), 1.0, 0.0)

    ch = TOKEN_CHUNK
    upper = jnp.where(lax.broadcasted_iota(I32, (ch, ch), 0) <= lax.broadcasted_iota(I32, (ch, ch), 1),
                      1.0, 0.0).astype(BF16)
    lane = lax.broadcasted_iota(I32, (ne, LANES), 1)

    offs_ref[...] = jnp.zeros_like(offs_ref)

    def chunk_step(j, run):
        off = pl.multiple_of(j * ch, ch)
        m = sel_ref[:, pl.ds(off, ch)]
        incl = _dot(m.astype(BF16), upper)
        p = jnp.where(m > 0.0, run + incl + (slot_base - 1.0), -1.0)
        pos_ref[:, pl.ds(off, ch)] = p.astype(I32)
        offs_ref[...] = jnp.where(lane == j, jnp.broadcast_to(run, (ne, LANES)).astype(I32), offs_ref[...])
        return run + incl[:, ch - 1:ch]

    lax.fori_loop(0, n // ch, chunk_step, jnp.zeros((ne, 1), F32))


def _select(aff, cap, slot_base):
    ne, n = aff.shape
    assert n % TOKEN_CHUNK == 0 and n // TOKEN_CHUNK <= LANES
    return pl.pallas_call(
        functools.partial(_select_kernel, cap=cap, slot_base=slot_base, idx_bits=max(1, (n - 1).bit_length())),
        out_shape=[jax.ShapeDtypeStruct((ne, n), I32), jax.ShapeDtypeStruct((ne, LANES), I32)],
        scratch_shapes=[pltpu.VMEM((ne, n), F32)],
        compiler_params=pltpu.CompilerParams(vmem_limit_bytes=VMEM_LIMIT),
        name="expert_select",
    )(aff)


GATHER_CHUNKS = 10
GATE_ROWS = 16


def _ffn_kernel(blo_ref, bhi_ref, pos_ref, aff_ref, x_hbm, wg_ref, wu_ref, wd_ref, ye_ref, xcat, sem, pcat, gcat,
                xe_ref, gate_ref, *, ts, sub):
    nj = pl.num_programs(1)
    j = pl.program_id(1)
    step = pl.program_id(0) * nj + j
    nsub = ts // sub
    nunits = pl.num_programs(0) * nj * nsub
    ch = TOKEN_CHUNK
    kb = GATHER_CHUNKS
    last_chunk = x_hbm.shape[0] // ch - 1

    def copy(c, which, k):
        return pltpu.make_async_copy(x_hbm.at[pl.ds(pl.multiple_of(c * ch, ch), ch), :],
                                     xcat.at[which, pl.ds(k * ch, ch), :], sem.at[which])

    def issue(lo, cnt, which):
        for k in range(kb):
            @pl.when(k < cnt)
            def _(k=k):
                copy(lo + k, which, k).start()

    def drain(cnt, which):
        for k in range(kb):
            @pl.when(k < cnt)
            def _(k=k):
                copy(0, which, k).wait()

    def unit_bounds(u):
        lo = blo_ref[u]
        return lo, bhi_ref[u] - lo

    @pl.when(step == 0)
    def _():
        xcat[...] = jnp.zeros_like(xcat)
        lo, n = unit_bounds(0)
        issue(lo, jnp.minimum(n, kb), 0)

    piece = lax.broadcasted_iota(I32, (GATE_ROWS, ch), 0)
    for t in range(nsub):
        u = step * nsub + t
        which = u % 2
        lo, n = unit_bounds(u)
        slot_id = lax.broadcasted_iota(I32, (sub, ch), 0) + (j * ts + t * sub)

        @pl.when(u + 1 < nunits)
        def _(u=u, which=which):
            lo1, n1 = unit_bounds(u + 1)
            issue(lo1, jnp.minimum(n1, kb), 1 - which)

        def build(first, cnt):
            for k in range(kb):
                c = jnp.minimum(first + k, last_chunk)
                off = pl.multiple_of(c * ch, ch)
                hit = (pos_ref[:, pl.ds(off, ch)] == slot_id) & (k < cnt)
                pcat[:, k * ch:(k + 1) * ch] = jnp.where(hit, 1.0, 0.0).astype(BF16)
                hi, mid, lo3 = _split3(aff_ref[:, pl.ds(off, ch)])
                gcat[:, k * ch:(k + 1) * ch] = jnp.where(
                    piece == 0, hi.astype(F32), jnp.where(piece == 1, mid.astype(F32),
                                                          jnp.where(piece == 2, lo3.astype(F32), 0.0))).astype(BF16)

        cnt0 = jnp.minimum(n, kb)
        build(lo, cnt0)
        drain(cnt0, which)
        rows = slice(t * sub, (t + 1) * sub)
        xe_ref[rows, :] = _dot(pcat[...], xcat[which])
        gate_ref[rows, :] = lax.dot_general(pcat[...], gcat[...], _NT, preferred_element_type=F32)

        def more(b, carry, which=which, lo=lo, n=n, rows=rows):
            first = lo + b * kb
            cnt = jnp.minimum(n - b * kb, kb)
            issue(first, cnt, which)
            build(first, cnt)
            drain(cnt, which)
            xe_ref[rows, :] += _dot(pcat[...], xcat[which])
            gate_ref[rows, :] += lax.dot_general(pcat[...], gcat[...], _NT, preferred_element_type=F32)
            return carry

        lax.fori_loop(1, (n + kb - 1) // kb, more, 0)

    xe = xe_ref[...].astype(BF16)
    g = _dot(xe, wg_ref[...])
    uu = _dot(xe, wu_ref[...])
    h = (g * _sigmoid(g)) * uu
    gate = jnp.sum(gate_ref[...], axis=1, keepdims=True)
    ye_ref[...] = (_dot(h.astype(BF16), wd_ref[...]) * gate).astype(BF16)


def _ffn(blo, bhi, pos3, aff3, x1b, wg, wu, wd, layer, cap_total, ts, sub):
    ntile = cap_total // ts
    nt = x1b.shape[0]
    per_expert = lambda e, j, *_: (e, 0, 0)
    layer_expert = lambda e, j, *_: (layer, e, 0, 0)
    return pl.pallas_call(
        functools.partial(_ffn_kernel, ts=ts, sub=sub),
        grid_spec=pltpu.PrefetchScalarGridSpec(
            num_scalar_prefetch=2,
            grid=(N_EXPERTS, ntile),
            in_specs=[pl.BlockSpec((None, 1, nt), per_expert),
                      pl.BlockSpec((None, 1, nt), per_expert),
                      pl.BlockSpec(memory_space=pl.ANY),
                      pl.BlockSpec((None, None, D_MODEL, EXPERT_FF), layer_expert),
                      pl.BlockSpec((None, None, D_MODEL, EXPERT_FF), layer_expert),
                      pl.BlockSpec((None, None, EXPERT_FF, D_MODEL), layer_expert)],
            out_specs=pl.BlockSpec((None, ts, D_MODEL), lambda e, j, *_: (e, j, 0)),
            scratch_shapes=[pltpu.VMEM((2, GATHER_CHUNKS * TOKEN_CHUNK, D_MODEL), BF16),
                            pltpu.SemaphoreType.DMA((2,)),
                            pltpu.VMEM((sub, GATHER_CHUNKS * TOKEN_CHUNK), BF16),
                            pltpu.VMEM((GATE_ROWS, GATHER_CHUNKS * TOKEN_CHUNK), BF16),
                            pltpu.VMEM((ts, D_MODEL), F32), pltpu.VMEM((ts, GATE_ROWS), F32)]),
        out_shape=jax.ShapeDtypeStruct((N_EXPERTS, cap_total, D_MODEL), BF16),
        compiler_params=_cparams("arbitrary", "arbitrary"),
        name="gather_expert_ffn",
    )(blo, bhi, pos3, aff3, x1b, wg, wu, wd)


def _combine_kernel(st_ref, nw_ref, x1_ref, post_ref, g2_ref, b2_ref, ye_hbm, *rest, cap_total, split):
    nout = 1 if split is None else 2
    out_refs, (ybuf, sem, xtra, sem_x, p_ref, acc_ref) = rest[:nout], rest[nout:]
    i = pl.program_id(0)
    ntiles = pl.num_programs(0)
    tm = x1_ref.shape[0]
    w = SLOT_WINDOW
    cur = i % 2

    def window(tile, e, k):
        lo = st_ref[tile * N_EXPERTS + e] + k * w
        return lo, pl.multiple_of(jnp.minimum(lo, cap_total - w), SLOT_ALIGN)

    def first_copy(tile, e, which):
        return pltpu.make_async_copy(ye_hbm.at[e, pl.ds(window(tile, e, 0)[1], w), :],
                                     ybuf.at[which, pl.ds(e * w, w), :], sem.at[which, e])

    @pl.when(i == 0)
    def _():
        for e in range(N_EXPERTS):
            first_copy(0, e, 0).start()

    @pl.when(i + 1 < ntiles)
    def _():
        for e in range(N_EXPERTS):
            first_copy(i + 1, e, 1 - cur).start()

    lane = lax.broadcasted_iota(I32, (tm, w), 1)

    def onehot(e, k):
        lo, start = window(i, e, k)
        pe = post_ref[:, e:e + 1]
        rel = jnp.where((pe >= lo) & (pe < start + w), pe - start, -1)
        return jnp.where(rel == lane, 1.0, 0.0).astype(BF16)

    for e in range(N_EXPERTS):
        p_ref[:, e * w:(e + 1) * w] = onehot(e, 0)
    for e in range(N_EXPERTS):
        first_copy(i, e, cur).wait()
    acc_ref[...] = _dot(p_ref[...], ybuf[cur])
    for e in range(N_EXPERTS):
        def extra(k, carry, e=e):
            cp = pltpu.make_async_copy(ye_hbm.at[e, pl.ds(window(i, e, k)[1], w), :], xtra, sem_x.at[0])
            cp.start()
            cp.wait()
            acc_ref[...] += _dot(onehot(e, k), xtra[...])
            return carry

        lax.fori_loop(1, nw_ref[i * N_EXPERTS + e], extra, 0)
    y = _layernorm(DEEPNORM_ALPHA * x1_ref[...] + acc_ref[...], g2_ref[...], b2_ref[...])
    if split is None:
        out_refs[0][...] = y
    else:
        @pl.when(i < split)
        def _():
            out_refs[0][...] = y

        @pl.when(i >= split)
        def _():
            out_refs[1][...] = y


def _combine(starts, nwin, x1, post, g2, b2, ye, tm, split=None):
    nt = x1.shape[0]
    ntiles = nt // tm
    cap_total = ye.shape[1]
    row = lambda i, *_: (i, 0)
    const = lambda i, *_: (0, 0)
    if split is None:
        out_specs = [pl.BlockSpec((tm, D_MODEL), row)]
        out_shape = [jax.ShapeDtypeStruct((nt, D_MODEL), F32)]
    else:
        out_specs = [pl.BlockSpec((tm, D_MODEL), lambda i, *_: (jnp.minimum(i, split - 1), 0)),
                     pl.BlockSpec((tm, D_MODEL), lambda i, *_: (jnp.maximum(i - split, 0), 0))]
        out_shape = [jax.ShapeDtypeStruct((split * tm, D_MODEL), F32),
                     jax.ShapeDtypeStruct(((ntiles - split) * tm, D_MODEL), F32)]
    return pl.pallas_call(
        functools.partial(_combine_kernel, cap_total=cap_total, split=split),
        grid_spec=pltpu.PrefetchScalarGridSpec(
            num_scalar_prefetch=2,
            grid=(ntiles,),
            in_specs=[pl.BlockSpec((tm, D_MODEL), row), pl.BlockSpec((tm, N_EXPERTS), row),
                      pl.BlockSpec(g2.shape, const), pl.BlockSpec(b2.shape, const),
                      pl.BlockSpec(memory_space=pl.ANY)],
            out_specs=out_specs,
            scratch_shapes=[pltpu.VMEM((2, N_EXPERTS * SLOT_WINDOW, D_MODEL), BF16),
                            pltpu.SemaphoreType.DMA((2, N_EXPERTS)),
                            pltpu.VMEM((SLOT_WINDOW, D_MODEL), BF16), pltpu.SemaphoreType.DMA((1,)),
                            pltpu.VMEM((tm, N_EXPERTS * SLOT_WINDOW), BF16), pltpu.VMEM((tm, D_MODEL), F32)]),
        out_shape=out_shape,
        compiler_params=_cparams("arbitrary"),
        name="combine_ln2",
    )(starts, nwin, x1, post, g2, b2, ye)


def _tile_bounds(offs_groups, caps, chunk_bases, ts):
    los, his = [], []
    for offs, cap, cbase in zip(offs_groups, caps, chunk_bases):
        nch = offs.shape[1]
        ends = jnp.concatenate([offs[:, 1:], jnp.full((N_EXPERTS, 1), cap, I32)], axis=1)
        s0 = (jnp.arange(cap // ts, dtype=I32) * ts)[None, :, None]
        los.append(cbase + jnp.sum((ends[:, None, :] <= s0).astype(I32), axis=2))
        his.append(cbase + jnp.sum((offs[:, None, :] < s0 + ts).astype(I32), axis=2))
        del nch
    return jnp.concatenate(los, axis=1).reshape(-1), jnp.concatenate(his, axis=1).reshape(-1)


def _window_bounds(offs_groups, caps, slot_bases, tm):
    starts, nwins = [], []
    per = tm // TOKEN_CHUNK
    for offs, cap, sbase in zip(offs_groups, caps, slot_bases):
        ends = jnp.concatenate([offs, jnp.full((N_EXPERTS, 1), cap, I32)], axis=1)
        first = ends[:, 0:-1:per] + sbase
        stop = ends[:, per::per] + sbase
        st = (first // SLOT_ALIGN) * SLOT_ALIGN
        nw = jnp.maximum((stop - st + SLOT_WINDOW - 1) // SLOT_WINDOW, 1)
        starts.append(st.T)
        nwins.append(nw.T)
    return jnp.concatenate(starts, axis=0).reshape(-1), jnp.concatenate(nwins, axis=0).reshape(-1)


def _prep_layer(l, p):
    bf = lambda a: a.astype(BF16)
    w_in = p["w_in"][l]
    c = 0
    cuts = {}
    for name, width in (("q", DA_WIDTH), ("k", DA_WIDTH), ("v", DA_WIDTH), ("cu", 2 * CONV_WIDTH),
                        ("g4", 2 * GLA_KEY_WIDTH + 2 * GLA_VAL_WIDTH), ("lr", 2 * GLA_GATE_RANK),
                        ("mg", N_BRANCH * D_MODEL)):
        cuts[name] = w_in[:, c:c + width]
        c += width
    gw2 = p["gla_gate_w2"][l]
    gw = jnp.zeros((LANES, 2 * GLA_KEY_WIDTH), F32)
    gw = gw.at[0:GLA_GATE_RANK, 0:GLA_KEY_WIDTH].set(gw2[0])
    gw = gw.at[GLA_GATE_RANK:2 * GLA_GATE_RANK, GLA_KEY_WIDTH:].set(gw2[1])
    lamp = jnp.zeros((8, LANES), F32)
    for r, nm in enumerate(("da_lam_q1", "da_lam_k1", "da_lam_q2", "da_lam_k2")):
        lamp = lamp.at[r, 0:DA_HEAD_DIM].set(p[nm][l])
    row = lambda a: a.reshape(1, -1)
    return {
        "wqk": bf(jnp.concatenate([cuts["q"], cuts["k"]], axis=1)), "wv": bf(cuts["v"]), "wcu": bf(cuts["cu"]),
        "wg4": bf(cuts["g4"]), "wlr": bf(jnp.pad(cuts["lr"], ((0, 0), (0, LANES - 2 * GLA_GATE_RANK)))),
        "gw": bf(gw), "gb": p["gla_gate_b"][l].reshape(1, -1),
        "lamp": lamp, "subln": row(p["da_subln_g"][l]),
        "conv_w": jnp.pad(p["conv_w"][l], ((0, 1), (0, 0))), "conv_b": row(p["conv_b"][l]),
        "conv_lg": row(p["conv_ln_g"][l]), "conv_lb": row(p["conv_ln_b"][l]),
        "ng": row(p["gla_norm_g"][l]), "wa": bf(p["da_w_o"][l]), "wc": bf(p["conv_w_o"][l]),
        "wl": bf(p["gla_w_o"][l]), "wmg": bf(cuts["mg"]), "wout": bf(p["w_out"][l]),
        "g1": row(p["ln1_g"][l]), "b1": row(p["ln1_b"][l]), "wr": bf(p["w_router"][l].T),
        "g2": row(p["ln2_g"][l]), "b2": row(p["ln2_b"][l]),
    }


def _rope_tables(seq):
    d = DA_HEAD_DIM
    inv = 1.0 / (ROPE_THETA ** (jnp.arange(0, d, 2, dtype=F32) / d))
    ang = jnp.arange(seq, dtype=F32)[:, None] * inv[None, :]
    c, s = jnp.cos(ang), jnp.sin(ang)
    cos = jnp.concatenate([c, c], axis=1)
    sin = jnp.concatenate([-s, s], axis=1)
    reps = LANES // d
    return jnp.tile(cos, (1, reps)), jnp.tile(sin, (1, reps))


def _tiles(seq, group_tokens, caps):
    g = functools.reduce(math.gcd, group_tokens)
    gc = functools.reduce(math.gcd, caps)
    return {
        "proj": min(512, seq), "attn": min(256, seq), "conv": min(256, seq), "gla": min(512, seq),
        "merge": min(512, seq), "ffn": min(512, gc), "gather": min(256, gc), "combine": min(512, g),
    }


def _encode(xs, p, depth):
    seq = xs[0].shape[1]
    nbatch = sum(x.shape[0] for x in xs)
    group_tokens = [x.shape[0] * seq for x in xs]
    caps = [CAPACITY_FACTOR * n // N_EXPERTS for n in group_tokens]
    slot_bases = [sum(caps[:i]) for i in range(len(caps))]
    token_bases = [sum(group_tokens[:i]) for i in range(len(caps))]
    chunk_bases = [t // TOKEN_CHUNK for t in token_bases]
    cap_total = sum(caps)
    t = _tiles(seq, group_tokens, caps)
    assert cap_total >= SLOT_WINDOW and all(c % t["ffn"] == 0 for c in caps)
    x = jnp.concatenate([x.reshape(-1, D_MODEL) for x in xs], axis=0)
    cos, sin = _rope_tables(seq)
    wgate, wup, wdown = (p[n].astype(BF16) for n in ("w_gate", "w_up", "w_down"))
    for l in range(depth):
        w = _prep_layer(l, p)
        lam_init = 0.8 - 0.6 * math.exp(-0.3 * l)
        q, k, v, u, gq, gk, gv, gr, gf, gb = _proj(x, cos, sin, w, seq, t["proj"])
        ao = _attn(q, k, v, w["lamp"], w["subln"], lam_init, nbatch, seq, t["attn"])
        ch = _conv(u, w["conv_w"], w["conv_b"], w["conv_lg"], w["conv_lb"], nbatch, seq, t["conv"])
        of, ob = _gla(gq, gk, gv, gf, gb, nbatch, seq, t["gla"])
        x1, x1b, aff = _merge(x, ao, ch, of, ob, gr, w, t["merge"])
        pos_g, offs_g = [], []
        for n, tb, cap, sb in zip(group_tokens, token_bases, caps, slot_bases):
            pos, offs = _select(aff[:, tb:tb + n], cap, sb)
            pos_g.append(pos)
            offs_g.append(offs[:, :n // TOKEN_CHUNK])
        pos = jnp.concatenate(pos_g, axis=1)
        blo, bhi = _tile_bounds(offs_g, caps, chunk_bases, t["gather"])
        ye = _ffn(blo, bhi, pos.reshape(N_EXPERTS, 1, -1), aff.reshape(N_EXPERTS, 1, -1), x1b,
                  wgate, wup, wdown, l, cap_total, t["ffn"], t["gather"])
        starts, nwin = _window_bounds(offs_g, caps, slot_bases, t["combine"])
        split = group_tokens[0] // t["combine"] if l == depth - 1 else None
        res = _combine(starts, nwin, x1, pos.T, w["g2"], w["b2"], ye, t["combine"], split=split)
        x = res[0]
    return tuple(xo.reshape(xg.shape) for xo, xg in zip(res, xs))


def kernel(x_prompt, x_sample, w_in, da_lam_q1, da_lam_k1, da_lam_q2, da_lam_k2, da_subln_g, da_w_o, conv_w, conv_b, conv_ln_g, conv_ln_b, conv_w_o, gla_gate_w2, gla_gate_b, gla_norm_g, gla_w_o, w_out, ln1_g, ln1_b, w_router, w_gate, w_up, w_down, ln2_g, ln2_b):
    p = dict(w_in=w_in, da_lam_q1=da_lam_q1, da_lam_k1=da_lam_k1, da_lam_q2=da_lam_q2, da_lam_k2=da_lam_k2,
             da_subln_g=da_subln_g, da_w_o=da_w_o, conv_w=conv_w, conv_b=conv_b, conv_ln_g=conv_ln_g,
             conv_ln_b=conv_ln_b, conv_w_o=conv_w_o, gla_gate_w2=gla_gate_w2, gla_gate_b=gla_gate_b,
             gla_norm_g=gla_norm_g, gla_w_o=gla_w_o, w_out=w_out, ln1_g=ln1_g, ln1_b=ln1_b, w_router=w_router,
             w_gate=w_gate, w_up=w_up, w_down=w_down, ln2_g=ln2_g, ln2_b=ln2_b)
    return _encode([x_prompt, x_sample], p, DEPTH)
```

```python
import functools
import math

import jax
import jax.numpy as jnp
from jax import lax
from jax.experimental import pallas as pl
from jax.experimental.pallas import tpu as pltpu

F32 = jnp.float32
BF16 = jnp.bfloat16
I32 = jnp.int32

D_MODEL = 1024
DEPTH = 4
DA_HEADS = 4
DA_HEAD_DIM = 64
DA_WIDTH = DA_HEADS * 2 * DA_HEAD_DIM
ROPE_THETA = 10000.0
CONV_WIDTH = 512
CONV_KERNEL = 31
GLA_HEADS = 4
GLA_DK = 64
GLA_DV = 128
GLA_KEY_WIDTH = GLA_HEADS * GLA_DK
GLA_VAL_WIDTH = GLA_HEADS * GLA_DV
GLA_GATE_RANK = 16
GLA_TAU = 16.0
GLA_CHUNK = 64
N_BRANCH = 3
N_EXPERTS = 16
EXPERT_FF = 2048
CAPACITY_FACTOR = 2
DEEPNORM_ALPHA = (2 * DEPTH) ** 0.25
NORM_EPS = 1e-5
LOG2E = 1.4426950408889634

LANES = 128
TOKEN_CHUNK = 256
SLOT_WINDOW = 128
SLOT_ALIGN = 16
CONV_HALO = 16
VMEM_LIMIT = 56 * 1024 * 1024

_NT = (((1,), (1,)), ((), ()))
_TN = (((0,), (0,)), ((), ()))


def _cparams(*sem):
    return pltpu.CompilerParams(dimension_semantics=sem, vmem_limit_bytes=VMEM_LIMIT)


def _dot(a, b):
    return jnp.dot(a, b, preferred_element_type=F32)


def _sigmoid(x):
    return 1.0 / (1.0 + jnp.exp(-x))


def _layernorm(y, g, b):
    mu = jnp.mean(y, axis=-1, keepdims=True)
    yc = y - mu
    var = jnp.mean(yc * yc, axis=-1, keepdims=True)
    return yc * lax.rsqrt(var + NORM_EPS) * g + b


def _proj_kernel(x_ref, cos_ref, sin_ref, wqk_ref, wv_ref, wcu_ref, wg4_ref, wlr_ref, gw_ref, gb_ref,
                 q_ref, k_ref, v_ref, u_ref, gq_ref, gk_ref, gv_ref, gr_ref, gf_ref, gbw_ref):
    xb = x_ref[...].astype(BF16)
    tm = xb.shape[0]
    qk = _dot(xb, wqk_ref[...])
    cos = cos_ref[...]
    sin = sin_ref[...]
    lane = lax.broadcasted_iota(I32, (tm, LANES), 1)
    first_half = (lane % DA_HEAD_DIM) < (DA_HEAD_DIM // 2)
    nqb = DA_WIDTH // LANES
    for cb in range(2 * nqb):
        xc = qk[:, cb * LANES:(cb + 1) * LANES]
        rot = jnp.where(first_half, pltpu.roll(xc, LANES - DA_HEAD_DIM // 2, 1), pltpu.roll(xc, DA_HEAD_DIM // 2, 1))
        r = xc * cos + rot * sin
        if cb < nqb:
            q_ref[:, cb * LANES:(cb + 1) * LANES] = (r * (DA_HEAD_DIM ** -0.5 * LOG2E)).astype(BF16)
        else:
            k_ref[:, (cb - nqb) * LANES:(cb - nqb + 1) * LANES] = r.astype(BF16)
    v_ref[...] = _dot(xb, wv_ref[...]).astype(BF16)
    cu = _dot(xb, wcu_ref[...])
    u_ref[...] = cu[:, :CONV_WIDTH] * _sigmoid(cu[:, CONV_WIDTH:])
    g4 = _dot(xb, wg4_ref[...])
    kw, vw = GLA_KEY_WIDTH, GLA_VAL_WIDTH
    gq_ref[...] = g4[:, :kw] * (GLA_DK ** -0.5)
    gk_ref[...] = g4[:, kw:2 * kw]
    gv_ref[...] = g4[:, 2 * kw:2 * kw + vw]
    gr_ref[...] = g4[:, 2 * kw + vw:]
    lr = _dot(xb, wlr_ref[...])
    gp = _dot(lr.astype(BF16), gw_ref[...]) + gb_ref[...]
    lsig = jnp.minimum(gp, 0.0) - jnp.log(1.0 + jnp.exp(-jnp.abs(gp)))
    lsig = lsig * (1.0 / GLA_TAU)
    gf_ref[...] = lsig[:, :kw]
    gbw_ref[...] = lsig[:, kw:]


def _proj(x, cos, sin, w, seq, tm):
    nt = x.shape[0]
    nseq = seq // tm
    row = lambda i: (i, 0)
    const = lambda i: (0, 0)
    full = lambda a: pl.BlockSpec(a.shape, const)
    outs = [(DA_WIDTH, BF16), (DA_WIDTH, BF16), (DA_WIDTH, BF16), (CONV_WIDTH, F32),
            (GLA_KEY_WIDTH, F32), (GLA_KEY_WIDTH, F32), (GLA_VAL_WIDTH, F32), (GLA_VAL_WIDTH, F32),
            (GLA_KEY_WIDTH, F32), (GLA_KEY_WIDTH, F32)]
    return pl.pallas_call(
        _proj_kernel,
        grid=(nt // tm,),
        in_specs=[pl.BlockSpec((tm, D_MODEL), row),
                  pl.BlockSpec((tm, LANES), lambda i: (i % nseq, 0)),
                  pl.BlockSpec((tm, LANES), lambda i: (i % nseq, 0)),
                  full(w["wqk"]), full(w["wv"]), full(w["wcu"]), full(w["wg4"]), full(w["wlr"]),
                  full(w["gw"]), full(w["gb"])],
        out_specs=[pl.BlockSpec((tm, c), row) for c, _ in outs],
        out_shape=[jax.ShapeDtypeStruct((nt, c), d) for c, d in outs],
        compiler_params=_cparams("parallel"),
        name="proj",
    )(x, cos, sin, w["wqk"], w["wv"], w["wcu"], w["wg4"], w["wlr"], w["gw"], w["gb"])


def _attn_kernel(lamp_ref, g_ref, q_ref, k_ref, v_ref, o_ref, *, lam_init):
    lp = lamp_ref[...]
    lam = (jnp.exp(jnp.sum(lp[0:1] * lp[1:2], axis=1, keepdims=True))
           - jnp.exp(jnp.sum(lp[2:3] * lp[3:4], axis=1, keepdims=True)) + lam_init)
    for h in range(DA_HEADS):
        cols = slice(h * LANES, (h + 1) * LANES)
        q = q_ref[:, cols]
        k = k_ref[:, cols]
        lane = lax.broadcasted_iota(I32, q.shape, 1)
        zero = jnp.zeros_like(q)
        s0 = lax.dot_general(jnp.where(lane < DA_HEAD_DIM, q, zero), k, _NT, preferred_element_type=F32)
        s1 = lax.dot_general(jnp.where(lane >= DA_HEAD_DIM, q, zero), k, _NT, preferred_element_type=F32)
        e0 = jnp.exp2(s0 - jnp.max(s0, axis=1, keepdims=True))
        e1 = jnp.exp2(s1 - jnp.max(s1, axis=1, keepdims=True))
        r0 = 1.0 / jnp.sum(e0, axis=1, keepdims=True)
        r1 = lam / jnp.sum(e1, axis=1, keepdims=True)
        p = e0 * r0 - e1 * r1
        o = _dot(p.astype(BF16), v_ref[:, cols])
        ms = jnp.mean(o * o, axis=1, keepdims=True)
        y = o * lax.rsqrt(ms + NORM_EPS) * g_ref[...] * (1.0 - lam_init)
        o_ref[:, cols] = y.astype(BF16)


def _attn(q, k, v, lamp, subln_g, lam_init, nbatch, seq, tq):
    nt = q.shape[0]
    nq = seq // tq
    return pl.pallas_call(
        functools.partial(_attn_kernel, lam_init=lam_init),
        grid=(nbatch, nq),
        in_specs=[pl.BlockSpec(lamp.shape, lambda b, i: (0, 0)),
                  pl.BlockSpec(subln_g.shape, lambda b, i: (0, 0)),
                  pl.BlockSpec((tq, DA_WIDTH), lambda b, i: (b * nq + i, 0)),
                  pl.BlockSpec((seq, DA_WIDTH), lambda b, i: (b, 0)),
                  pl.BlockSpec((seq, DA_WIDTH), lambda b, i: (b, 0))],
        out_specs=pl.BlockSpec((tq, DA_WIDTH), lambda b, i: (b * nq + i, 0)),
        out_shape=jax.ShapeDtypeStruct((nt, DA_WIDTH), BF16),
        compiler_params=_cparams("parallel", "parallel"),
        name="diff_attn",
    )(lamp, subln_g, q, k, v)


CONV_ROWS = 32


SUBLANES = 8


def _conv_kernel(prev_ref, cur_ref, next_ref, w_ref, cb_ref, lg_ref, lb_ref, o_ref, scr, shifted, *, ts, nblk):
    i = pl.program_id(1)
    scr[0:CONV_HALO, :] = jnp.where(i > 0, prev_ref[...], 0.0)
    scr[CONV_HALO:CONV_HALO + ts, :] = cur_ref[...]
    scr[CONV_HALO + ts:2 * CONV_HALO + ts, :] = jnp.where(i < nblk - 1, next_ref[...], 0.0)
    span = ts + 2 * CONV_HALO - SUBLANES
    for s in range(1, SUBLANES):
        shifted[s - 1] = scr[s:s + span, :]
    w = w_ref[...]
    base = CONV_HALO - CONV_KERNEL // 2
    for rb in range(ts // CONV_ROWS):
        r0 = rb * CONV_ROWS
        acc = jnp.zeros((CONV_ROWS, CONV_WIDTH), F32)
        for t in range(CONV_KERNEL):
            phase = (base + t) % SUBLANES
            row = r0 + base + t - phase
            if phase == 0:
                xs = scr[row:row + CONV_ROWS, :]
            else:
                xs = shifted[phase - 1, row:row + CONV_ROWS, :]
            acc = acc + w[t:t + 1, :] * xs
        h = _layernorm(acc + cb_ref[...], lg_ref[...], lb_ref[...])
        o_ref[r0:r0 + CONV_ROWS, :] = (h * _sigmoid(h)).astype(BF16)


def _conv(u, w, cb, lg, lb, nbatch, seq, ts):
    nt = u.shape[0]
    nblk = seq // ts
    hb = ts // CONV_HALO
    nhalo = nt // CONV_HALO
    const = lambda b, i: (0, 0)
    return pl.pallas_call(
        functools.partial(_conv_kernel, ts=ts, nblk=nblk),
        grid=(nbatch, nblk),
        in_specs=[pl.BlockSpec((CONV_HALO, CONV_WIDTH), lambda b, i: (jnp.maximum((b * nblk + i) * hb - 1, 0), 0)),
                  pl.BlockSpec((ts, CONV_WIDTH), lambda b, i: (b * nblk + i, 0)),
                  pl.BlockSpec((CONV_HALO, CONV_WIDTH),
                               lambda b, i: (jnp.minimum((b * nblk + i + 1) * hb, nhalo - 1), 0)),
                  pl.BlockSpec(w.shape, const), pl.BlockSpec(cb.shape, const),
                  pl.BlockSpec(lg.shape, const), pl.BlockSpec(lb.shape, const)],
        out_specs=pl.BlockSpec((ts, CONV_WIDTH), lambda b, i: (b * nblk + i, 0)),
        out_shape=jax.ShapeDtypeStruct((nt, CONV_WIDTH), BF16),
        scratch_shapes=[pltpu.VMEM((ts + 2 * CONV_HALO, CONV_WIDTH), F32),
                        pltpu.VMEM((SUBLANES - 1, ts + 2 * CONV_HALO - SUBLANES, CONV_WIDTH), F32)],
        compiler_params=_cparams("parallel", "parallel"),
        name="conformer_conv",
    )(u, u, u, w, cb, lg, lb)


def _split3(g):
    hi = g.astype(BF16)
    r1 = g - hi.astype(F32)
    mid = r1.astype(BF16)
    lo = (r1 - mid.astype(F32)).astype(BF16)
    return hi, mid, lo


def _gla_kernel(qf_ref, kf_ref, vf_ref, gf_ref, qb_ref, kb_ref, vb_ref, gb_ref, of_ref, ob_ref, st_ref, *, tc):
    @pl.when(pl.program_id(1) == 0)
    def _():
        st_ref[...] = jnp.zeros_like(st_ref)

    c = GLA_CHUNK
    nchunk = tc // c
    npair = GLA_HEADS // 2
    refs = ((qf_ref, kf_ref, vf_ref, gf_ref, of_ref), (qb_ref, kb_ref, vb_ref, gb_ref, ob_ref))
    r = lax.broadcasted_iota(I32, (c, c), 0)
    cc = lax.broadcasted_iota(I32, (c, c), 1)
    r2 = lax.broadcasted_iota(I32, (c, 2 * c), 0)
    c2 = lax.broadcasted_iota(I32, (c, 2 * c), 1) % c
    keep = (cc <= r, cc >= r)
    keep2 = (c2 <= r2, c2 >= r2)
    tri = tuple(jnp.where(m, 1.0, 0.0).astype(BF16) for m in keep)
    head0 = lax.broadcasted_iota(I32, (c, LANES), 1) < GLA_DK
    vhead0 = lax.broadcasted_iota(I32, (c, 2 * GLA_DV), 1) < GLA_DV
    same_head = ((lax.broadcasted_iota(I32, (2 * GLA_DV, LANES), 0) < GLA_DV)
                 == (lax.broadcasted_iota(I32, (2 * GLA_DV, LANES), 1) < GLA_DK))
    items = [(d, p, s) for s in range(nchunk) for p in range(npair) for d in range(2)]

    def where_(d, p, s):
        ci = s if d == 0 else nchunk - 1 - s
        return slice(ci * c, (ci + 1) * c), slice(p * LANES, (p + 1) * LANES), slice(p * 2 * GLA_DV, (p + 1) * 2 * GLA_DV)

    b3 = {}
    for it in items:
        d, p, s = it
        rows, kl, _ = where_(*it)
        b3[it] = _dot(tri[d], jnp.concatenate(_split3(refs[d][3][rows, kl]), axis=1))
    qt, decay, amat, kvm, vbd = {}, {}, {}, {}, {}
    for it in items:
        d, p, s = it
        rows, kl, vl = where_(*it)
        b = b3[it][:, :LANES] + b3[it][:, LANES:2 * LANES] + b3[it][:, 2 * LANES:]
        b_last = b[0:1, :] if d == 1 else b[c - 1:c, :]
        q = refs[d][0][rows, kl]
        k = refs[d][1][rows, kl]
        qt[it] = (q * jnp.exp(b)).astype(BF16)
        kt = (k * jnp.exp(-b)).astype(BF16)
        kd = (k * jnp.exp(b_last - b)).astype(BF16)
        decay[it] = jnp.exp(b_last)
        zk = jnp.zeros_like(kt)
        kstack = jnp.concatenate([jnp.where(head0, kt, zk), jnp.where(head0, zk, kt)], axis=0)
        a = lax.dot_general(qt[it], kstack, _NT, preferred_element_type=F32)
        amat[it] = jnp.where(keep2[d], a, 0.0).astype(BF16)
        v = refs[d][2][rows, vl].astype(BF16)
        zv = jnp.zeros_like(v)
        vbd[it] = jnp.concatenate([jnp.where(vhead0, v, zv), jnp.where(vhead0, zv, v)], axis=0)
        kv = lax.dot_general(v, kd, _TN, preferred_element_type=F32)
        kvm[it] = jnp.where(same_head, kv, 0.0)
    pre = {}
    for d in range(2):
        for p in range(npair):
            st = st_ref[d, p]
            for s in range(nchunk):
                it = (d, p, s)
                pre[it] = st.astype(BF16)
                st = st * decay[it] + kvm[it]
            st_ref[d, p] = st
    for it in items:
        d, p, s = it
        rows, _, vl = where_(*it)
        refs[d][4][rows, vl] = (_dot(amat[it], vbd[it])
                                + lax.dot_general(qt[it], pre[it], _NT, preferred_element_type=F32))


def _gla(gq, gk, gv, gf, gb, nbatch, seq, tc):
    nt = gq.shape[0]
    nj = seq // tc
    fwd = lambda b, j: (b * nj + j, 0)
    bwd = lambda b, j: (b * nj + nj - 1 - j, 0)
    kspec = lambda m: pl.BlockSpec((tc, GLA_KEY_WIDTH), m)
    vspec = lambda m: pl.BlockSpec((tc, GLA_VAL_WIDTH), m)
    return pl.pallas_call(
        functools.partial(_gla_kernel, tc=tc),
        grid=(nbatch, nj),
        in_specs=[kspec(fwd), kspec(fwd), vspec(fwd), kspec(fwd), kspec(bwd), kspec(bwd), vspec(bwd), kspec(bwd)],
        out_specs=[vspec(fwd), vspec(bwd)],
        out_shape=[jax.ShapeDtypeStruct((nt, GLA_VAL_WIDTH), F32)] * 2,
        scratch_shapes=[pltpu.VMEM((2, GLA_HEADS // 2, 2 * GLA_DV, LANES), F32)],
        compiler_params=_cparams("parallel", "arbitrary"),
        name="bi_gla",
    )(gq, gk, gv, gf, gq, gk, gv, gb)


def _merge_kernel(x_ref, ao_ref, ch_ref, of_ref, ob_ref, gr_ref, ng_ref, wa_ref, wc_ref, wl_ref, wmg_ref, wout_ref,
                  g1_ref, b1_ref, wr_ref, x1_ref, x1b_ref, aff_ref):
    x = x_ref[...]
    xb = x.astype(BF16)
    a = _dot(ao_ref[...], wa_ref[...])
    c = _dot(ch_ref[...], wc_ref[...])
    o = of_ref[...] + ob_ref[...]
    r = gr_ref[...]
    ng = ng_ref[...]
    parts = []
    for h in range(GLA_HEADS):
        oh = o[:, h * GLA_DV:(h + 1) * GLA_DV]
        rh = r[:, h * GLA_DV:(h + 1) * GLA_DV]
        yh = oh * lax.rsqrt(jnp.mean(oh * oh, axis=1, keepdims=True) + NORM_EPS) * ng
        parts.append((yh * (rh * _sigmoid(rh))).astype(BF16))
    l = _dot(jnp.concatenate(parts, axis=1), wl_ref[...])
    m = _sigmoid(_dot(xb, wmg_ref[:, 0:D_MODEL])) * a
    m = m + _sigmoid(_dot(xb, wmg_ref[:, D_MODEL:2 * D_MODEL])) * c
    m = m + _sigmoid(_dot(xb, wmg_ref[:, 2 * D_MODEL:3 * D_MODEL])) * l
    y = DEEPNORM_ALPHA * x + _dot(m.astype(BF16), wout_ref[...])
    x1 = _layernorm(y, g1_ref[...], b1_ref[...])
    x1_ref[...] = x1
    x1b = x1.astype(BF16)
    x1b_ref[...] = x1b
    logits = lax.dot_general(wr_ref[...], x1b, _NT, preferred_element_type=F32)
    ex = jnp.exp(logits - jnp.max(logits, axis=0, keepdims=True))
    aff_ref[...] = ex / jnp.sum(ex, axis=0, keepdims=True)


def _merge(x, ao, ch, of, ob, gr, w, tm):
    nt = x.shape[0]
    row = lambda i: (i, 0)
    const = lambda i: (0, 0)
    full = lambda a: pl.BlockSpec(a.shape, const)
    names = ["ng", "wa", "wc", "wl", "wmg", "wout", "g1", "b1", "wr"]
    return pl.pallas_call(
        _merge_kernel,
        grid=(nt // tm,),
        in_specs=[pl.BlockSpec((tm, D_MODEL), row), pl.BlockSpec((tm, DA_WIDTH), row),
                  pl.BlockSpec((tm, CONV_WIDTH), row), pl.BlockSpec((tm, GLA_VAL_WIDTH), row),
                  pl.BlockSpec((tm, GLA_VAL_WIDTH), row), pl.BlockSpec((tm, GLA_VAL_WIDTH), row)]
                 + [full(w[n]) for n in names],
        out_specs=[pl.BlockSpec((tm, D_MODEL), row), pl.BlockSpec((tm, D_MODEL), row),
                   pl.BlockSpec((N_EXPERTS, tm), lambda i: (0, i))],
        out_shape=[jax.ShapeDtypeStruct((nt, D_MODEL), F32), jax.ShapeDtypeStruct((nt, D_MODEL), BF16),
                   jax.ShapeDtypeStruct((N_EXPERTS, nt), F32)],
        compiler_params=_cparams("parallel"),
        name="merge_ln1_router",
    )(x, ao, ch, of, ob, gr, *[w[n] for n in names])


def _select_kernel(aff_ref, pos_ref, offs_ref, sel_ref, *, cap, slot_base, idx_bits):
    ne, n = aff_ref.shape
    bits = pltpu.bitcast(aff_ref[...], I32)
    capf = float(cap)

    def count(mask):
        return jnp.sum(jnp.where(mask, 1.0, 0.0), axis=1, keepdims=True)

    def value_step(i, t):
        cand = t | jnp.left_shift(jnp.int32(1), 30 - i)
        return jnp.where(count(bits >= cand) >= capf, cand, t)

    thr = lax.fori_loop(0, 31, value_step, jnp.zeros((ne, 1), I32))
    above = bits > thr
    tie = bits == thr
    need = capf - count(above)
    idx = lax.broadcasted_iota(I32, (ne, n), 1)

    def index_step(i, ans):
        cand = ans | jnp.left_shift(jnp.int32(1), idx_bits - 1 - i)
        return jnp.where(count(tie & (idx < cand)) < need, cand, ans)

    last_tie = lax.fori_loop(0, idx_bits, index_step, jnp.zeros((ne, 1), I32))
    sel_ref[...] = jnp.where(above | (tie & (idx <= last_tie)), 1.0, 0.0)

    ch = TOKEN_CHUNK
    upper = jnp.where(lax.broadcasted_iota(I32, (ch, ch), 0) <= lax.broadcasted_iota(I32, (ch, ch), 1),
                      1.0, 0.0).astype(BF16)
    lane = lax.broadcasted_iota(I32, (ne, LANES), 1)

    offs_ref[...] = jnp.zeros_like(offs_ref)

    def chunk_step(j, run):
        off = pl.multiple_of(j * ch, ch)
        m = sel_ref[:, pl.ds(off, ch)]
        incl = _dot(m.astype(BF16), upper)
        p = jnp.where(m > 0.0, run + incl + (slot_base - 1.0), -1.0)
        pos_ref[:, pl.ds(off, ch)] = p.astype(I32)
        offs_ref[...] = jnp.where(lane == j, jnp.broadcast_to(run, (ne, LANES)).astype(I32), offs_ref[...])
        return run + incl[:, ch - 1:ch]

    lax.fori_loop(0, n // ch, chunk_step, jnp.zeros((ne, 1), F32))


def _select(aff, cap, slot_base):
    ne, n = aff.shape
    assert n % TOKEN_CHUNK == 0 and n // TOKEN_CHUNK <= LANES
    return pl.pallas_call(
        functools.partial(_select_kernel, cap=cap, slot_base=slot_base, idx_bits=max(1, (n - 1).bit_length())),
        out_shape=[jax.ShapeDtypeStruct((ne, n), I32), jax.ShapeDtypeStruct((ne, LANES), I32)],
        scratch_shapes=[pltpu.VMEM((ne, n), F32)],
        compiler_params=pltpu.CompilerParams(vmem_limit_bytes=VMEM_LIMIT),
        name="expert_select",
    )(aff)


GATHER_CHUNKS = 10


def _ffn_kernel(blo_ref, bhi_ref, pos_ref, aff_ref, x_hbm, wg_ref, wu_ref, wd_ref, ye_ref, xcat, sem, pcat,
                xe_ref, gate_ref, *, ts, sub):
    nj = pl.num_programs(1)
    j = pl.program_id(1)
    step = pl.program_id(0) * nj + j
    nsub = ts // sub
    ch = TOKEN_CHUNK
    kb = GATHER_CHUNKS
    last_chunk = x_hbm.shape[0] // ch - 1

    def copy(c, which, k):
        return pltpu.make_async_copy(x_hbm.at[pl.ds(pl.multiple_of(c * ch, ch), ch), :],
                                     xcat.at[which, pl.ds(k * ch, ch), :], sem.at[which])

    def issue(lo, cnt, which):
        for k in range(kb):
            @pl.when(k < cnt)
            def _(k=k):
                copy(lo + k, which, k).start()

    def drain(cnt, which):
        for k in range(kb):
            @pl.when(k < cnt)
            def _(k=k):
                copy(0, which, k).wait()

    def unit_bounds(u):
        lo = blo_ref[u]
        return lo, bhi_ref[u] - lo

    @pl.when(step == 0)
    def _():
        xcat[...] = jnp.zeros_like(xcat)
        lo, n = unit_bounds(0)
        issue(lo, jnp.minimum(n, kb), 0)

    def set_of(u, t):
        return t % 2 if nsub % 2 == 0 else u % 2

    def build(t, first, cnt, slot_id, rows):
        for k in range(kb):
            c = jnp.minimum(first + k, last_chunk)
            off = pl.multiple_of(c * ch, ch)
            hit = (pos_ref[:, pl.ds(off, ch)] == slot_id) & (k < cnt)
            pcat[t, :, k * ch:(k + 1) * ch] = jnp.where(hit, 1.0, 0.0).astype(BF16)
            gate_ref[rows, :] += jnp.where(hit, aff_ref[:, pl.ds(off, ch)], 0.0)

    units = []
    for t in range(nsub):
        u = step * nsub + t
        which = set_of(u, t)
        lo, n = unit_bounds(u)
        slot_id = lax.broadcasted_iota(I32, (sub, ch), 0) + (j * ts + t * sub)
        rows = slice(t * sub, (t + 1) * sub)
        cnt0 = jnp.minimum(n, kb)
        gate_ref[rows, :] = jnp.zeros((sub, ch), F32)
        build(t, lo, cnt0, slot_id, rows)
        if t + 1 < nsub:
            lo1, n1 = unit_bounds(u + 1)
            issue(lo1, jnp.minimum(n1, kb), 1 - which)
        drain(cnt0, which)
        xe_ref[rows, :] = _dot(pcat[t], xcat[which])
        units.append((t, which, lo, n, slot_id, rows))

    for t, which, lo, n, slot_id, rows in units:
        def more(b, carry, t=t, which=which, lo=lo, n=n, slot_id=slot_id, rows=rows):
            first = lo + b * kb
            cnt = jnp.minimum(n - b * kb, kb)
            issue(first, cnt, which)
            build(t, first, cnt, slot_id, rows)
            drain(cnt, which)
            xe_ref[rows, :] += _dot(pcat[t], xcat[which])
            return carry

        lax.fori_loop(1, (n + kb - 1) // kb, more, 0)

    @pl.when(step + 1 < pl.num_programs(0) * nj)
    def _():
        un = (step + 1) * nsub
        lo1, n1 = unit_bounds(un)
        issue(lo1, jnp.minimum(n1, kb), set_of(un, 0))

    xe = xe_ref[...].astype(BF16)
    g = _dot(xe, wg_ref[...])
    uu = _dot(xe, wu_ref[...])
    h = (g * _sigmoid(g)) * uu
    gate = jnp.sum(gate_ref[...], axis=1, keepdims=True)
    ye_ref[...] = (_dot(h.astype(BF16), wd_ref[...]) * gate).astype(BF16)


def _ffn(blo, bhi, pos3, aff3, x1b, wg, wu, wd, layer, cap_total, ts, sub):
    ntile = cap_total // ts
    nt = x1b.shape[0]
    per_expert = lambda e, j, *_: (e, 0, 0)
    layer_expert = lambda e, j, *_: (layer, e, 0, 0)
    return pl.pallas_call(
        functools.partial(_ffn_kernel, ts=ts, sub=sub),
        grid_spec=pltpu.PrefetchScalarGridSpec(
            num_scalar_prefetch=2,
            grid=(N_EXPERTS, ntile),
            in_specs=[pl.BlockSpec((None, 1, nt), per_expert),
                      pl.BlockSpec((None, 1, nt), per_expert),
                      pl.BlockSpec(memory_space=pl.ANY),
                      pl.BlockSpec((None, None, D_MODEL, EXPERT_FF), layer_expert),
                      pl.BlockSpec((None, None, D_MODEL, EXPERT_FF), layer_expert),
                      pl.BlockSpec((None, None, EXPERT_FF, D_MODEL), layer_expert)],
            out_specs=pl.BlockSpec((None, ts, D_MODEL), lambda e, j, *_: (e, j, 0)),
            scratch_shapes=[pltpu.VMEM((2, GATHER_CHUNKS * TOKEN_CHUNK, D_MODEL), BF16),
                            pltpu.SemaphoreType.DMA((2,)),
                            pltpu.VMEM((ts // sub, sub, GATHER_CHUNKS * TOKEN_CHUNK), BF16),
                            pltpu.VMEM((ts, D_MODEL), F32), pltpu.VMEM((ts, TOKEN_CHUNK), F32)]),
        out_shape=jax.ShapeDtypeStruct((N_EXPERTS, cap_total, D_MODEL), BF16),
        compiler_params=_cparams("arbitrary", "arbitrary"),
        name="gather_expert_ffn",
    )(blo, bhi, pos3, aff3, x1b, wg, wu, wd)


def _combine_kernel(st_ref, nw_ref, x1_ref, post_ref, g2_ref, b2_ref, ye_hbm, *rest, cap_total, split):
    nout = 1 if split is None else 2
    out_refs, (ybuf, sem, xtra, sem_x, p_ref, acc_ref) = rest[:nout], rest[nout:]
    i = pl.program_id(0)
    ntiles = pl.num_programs(0)
    tm = x1_ref.shape[0]
    w = SLOT_WINDOW
    cur = i % 2

    def window(tile, e, k):
        lo = st_ref[tile * N_EXPERTS + e] + k * w
        return lo, pl.multiple_of(jnp.minimum(lo, cap_total - w), SLOT_ALIGN)

    def first_copy(tile, e, which):
        return pltpu.make_async_copy(ye_hbm.at[e, pl.ds(window(tile, e, 0)[1], w), :],
                                     ybuf.at[which, pl.ds(e * w, w), :], sem.at[which, e])

    @pl.when(i == 0)
    def _():
        for e in range(N_EXPERTS):
            first_copy(0, e, 0).start()

    @pl.when(i + 1 < ntiles)
    def _():
        for e in range(N_EXPERTS):
            first_copy(i + 1, e, 1 - cur).start()

    lane = lax.broadcasted_iota(I32, (tm, w), 1)

    def onehot(e, k):
        lo, start = window(i, e, k)
        pe = post_ref[:, e:e + 1]
        rel = jnp.where((pe >= lo) & (pe < start + w), pe - start, -1)
        return jnp.where(rel == lane, 1.0, 0.0).astype(BF16)

    for e in range(N_EXPERTS):
        p_ref[:, e * w:(e + 1) * w] = onehot(e, 0)
    for e in range(N_EXPERTS):
        first_copy(i, e, cur).wait()
    acc_ref[...] = _dot(p_ref[...], ybuf[cur])
    for e in range(N_EXPERTS):
        def extra(k, carry, e=e):
            cp = pltpu.make_async_copy(ye_hbm.at[e, pl.ds(window(i, e, k)[1], w), :], xtra, sem_x.at[0])
            cp.start()
            cp.wait()
            acc_ref[...] += _dot(onehot(e, k), xtra[...])
            return carry

        lax.fori_loop(1, nw_ref[i * N_EXPERTS + e], extra, 0)
    y = _layernorm(DEEPNORM_ALPHA * x1_ref[...] + acc_ref[...], g2_ref[...], b2_ref[...])
    if split is None:
        out_refs[0][...] = y
    else:
        @pl.when(i < split)
        def _():
            out_refs[0][...] = y

        @pl.when(i >= split)
        def _():
            out_refs[1][...] = y


def _combine(starts, nwin, x1, post, g2, b2, ye, tm, split=None):
    nt = x1.shape[0]
    ntiles = nt // tm
    cap_total = ye.shape[1]
    row = lambda i, *_: (i, 0)
    const = lambda i, *_: (0, 0)
    if split is None:
        out_specs = [pl.BlockSpec((tm, D_MODEL), row)]
        out_shape = [jax.ShapeDtypeStruct((nt, D_MODEL), F32)]
    else:
        out_specs = [pl.BlockSpec((tm, D_MODEL), lambda i, *_: (jnp.minimum(i, split - 1), 0)),
                     pl.BlockSpec((tm, D_MODEL), lambda i, *_: (jnp.maximum(i - split, 0), 0))]
        out_shape = [jax.ShapeDtypeStruct((split * tm, D_MODEL), F32),
                     jax.ShapeDtypeStruct(((ntiles - split) * tm, D_MODEL), F32)]
    return pl.pallas_call(
        functools.partial(_combine_kernel, cap_total=cap_total, split=split),
        grid_spec=pltpu.PrefetchScalarGridSpec(
            num_scalar_prefetch=2,
            grid=(ntiles,),
            in_specs=[pl.BlockSpec((tm, D_MODEL), row), pl.BlockSpec((tm, N_EXPERTS), row),
                      pl.BlockSpec(g2.shape, const), pl.BlockSpec(b2.shape, const),
                      pl.BlockSpec(memory_space=pl.ANY)],
            out_specs=out_specs,
            scratch_shapes=[pltpu.VMEM((2, N_EXPERTS * SLOT_WINDOW, D_MODEL), BF16),
                            pltpu.SemaphoreType.DMA((2, N_EXPERTS)),
                            pltpu.VMEM((SLOT_WINDOW, D_MODEL), BF16), pltpu.SemaphoreType.DMA((1,)),
                            pltpu.VMEM((tm, N_EXPERTS * SLOT_WINDOW), BF16), pltpu.VMEM((tm, D_MODEL), F32)]),
        out_shape=out_shape,
        compiler_params=_cparams("arbitrary"),
        name="combine_ln2",
    )(starts, nwin, x1, post, g2, b2, ye)


def _tile_bounds(offs_groups, caps, chunk_bases, ts):
    los, his = [], []
    for offs, cap, cbase in zip(offs_groups, caps, chunk_bases):
        nch = offs.shape[1]
        ends = jnp.concatenate([offs[:, 1:], jnp.full((N_EXPERTS, 1), cap, I32)], axis=1)
        s0 = (jnp.arange(cap // ts, dtype=I32) * ts)[None, :, None]
        los.append(cbase + jnp.sum((ends[:, None, :] <= s0).astype(I32), axis=2))
        his.append(cbase + jnp.sum((offs[:, None, :] < s0 + ts).astype(I32), axis=2))
        del nch
    return jnp.concatenate(los, axis=1).reshape(-1), jnp.concatenate(his, axis=1).reshape(-1)


def _window_bounds(offs_groups, caps, slot_bases, tm):
    starts, nwins = [], []
    per = tm // TOKEN_CHUNK
    for offs, cap, sbase in zip(offs_groups, caps, slot_bases):
        ends = jnp.concatenate([offs, jnp.full((N_EXPERTS, 1), cap, I32)], axis=1)
        first = ends[:, 0:-1:per] + sbase
        stop = ends[:, per::per] + sbase
        st = (first // SLOT_ALIGN) * SLOT_ALIGN
        nw = jnp.maximum((stop - st + SLOT_WINDOW - 1) // SLOT_WINDOW, 1)
        starts.append(st.T)
        nwins.append(nw.T)
    return jnp.concatenate(starts, axis=0).reshape(-1), jnp.concatenate(nwins, axis=0).reshape(-1)


def _prep_layer(l, p):
    bf = lambda a: a.astype(BF16)
    w_in = p["w_in"][l]
    c = 0
    cuts = {}
    for name, width in (("q", DA_WIDTH), ("k", DA_WIDTH), ("v", DA_WIDTH), ("cu", 2 * CONV_WIDTH),
                        ("g4", 2 * GLA_KEY_WIDTH + 2 * GLA_VAL_WIDTH), ("lr", 2 * GLA_GATE_RANK),
                        ("mg", N_BRANCH * D_MODEL)):
        cuts[name] = w_in[:, c:c + width]
        c += width
    gw2 = p["gla_gate_w2"][l]
    gw = jnp.zeros((LANES, 2 * GLA_KEY_WIDTH), F32)
    gw = gw.at[0:GLA_GATE_RANK, 0:GLA_KEY_WIDTH].set(gw2[0])
    gw = gw.at[GLA_GATE_RANK:2 * GLA_GATE_RANK, GLA_KEY_WIDTH:].set(gw2[1])
    lamp = jnp.zeros((8, LANES), F32)
    for r, nm in enumerate(("da_lam_q1", "da_lam_k1", "da_lam_q2", "da_lam_k2")):
        lamp = lamp.at[r, 0:DA_HEAD_DIM].set(p[nm][l])
    row = lambda a: a.reshape(1, -1)
    return {
        "wqk": bf(jnp.concatenate([cuts["q"], cuts["k"]], axis=1)), "wv": bf(cuts["v"]), "wcu": bf(cuts["cu"]),
        "wg4": bf(cuts["g4"]), "wlr": bf(jnp.pad(cuts["lr"], ((0, 0), (0, LANES - 2 * GLA_GATE_RANK)))),
        "gw": bf(gw), "gb": p["gla_gate_b"][l].reshape(1, -1),
        "lamp": lamp, "subln": row(p["da_subln_g"][l]),
        "conv_w": jnp.pad(p["conv_w"][l], ((0, 1), (0, 0))), "conv_b": row(p["conv_b"][l]),
        "conv_lg": row(p["conv_ln_g"][l]), "conv_lb": row(p["conv_ln_b"][l]),
        "ng": row(p["gla_norm_g"][l]), "wa": bf(p["da_w_o"][l]), "wc": bf(p["conv_w_o"][l]),
        "wl": bf(p["gla_w_o"][l]), "wmg": bf(cuts["mg"]), "wout": bf(p["w_out"][l]),
        "g1": row(p["ln1_g"][l]), "b1": row(p["ln1_b"][l]), "wr": bf(p["w_router"][l].T),
        "g2": row(p["ln2_g"][l]), "b2": row(p["ln2_b"][l]),
    }


def _rope_tables(seq):
    d = DA_HEAD_DIM
    inv = 1.0 / (ROPE_THETA ** (jnp.arange(0, d, 2, dtype=F32) / d))
    ang = jnp.arange(seq, dtype=F32)[:, None] * inv[None, :]
    c, s = jnp.cos(ang), jnp.sin(ang)
    cos = jnp.concatenate([c, c], axis=1)
    sin = jnp.concatenate([-s, s], axis=1)
    reps = LANES // d
    return jnp.tile(cos, (1, reps)), jnp.tile(sin, (1, reps))


def _tiles(seq, group_tokens, caps):
    g = functools.reduce(math.gcd, group_tokens)
    gc = functools.reduce(math.gcd, caps)
    return {
        "proj": min(512, seq), "attn": min(256, seq), "conv": min(256, seq), "gla": min(512, seq),
        "merge": min(512, seq), "ffn": min(512, gc), "gather": min(256, gc), "combine": min(512, g),
    }


def _encode(xs, p, depth):
    seq = xs[0].shape[1]
    nbatch = sum(x.shape[0] for x in xs)
    group_tokens = [x.shape[0] * seq for x in xs]
    caps = [CAPACITY_FACTOR * n // N_EXPERTS for n in group_tokens]
    slot_bases = [sum(caps[:i]) for i in range(len(caps))]
    token_bases = [sum(group_tokens[:i]) for i in range(len(caps))]
    chunk_bases = [t // TOKEN_CHUNK for t in token_bases]
    cap_total = sum(caps)
    t = _tiles(seq, group_tokens, caps)
    assert cap_total >= SLOT_WINDOW and all(c % t["ffn"] == 0 for c in caps)
    x = jnp.concatenate([x.reshape(-1, D_MODEL) for x in xs], axis=0)
    cos, sin = _rope_tables(seq)
    wgate, wup, wdown = (p[n].astype(BF16) for n in ("w_gate", "w_up", "w_down"))
    for l in range(depth):
        w = _prep_layer(l, p)
        lam_init = 0.8 - 0.6 * math.exp(-0.3 * l)
        q, k, v, u, gq, gk, gv, gr, gf, gb = _proj(x, cos, sin, w, seq, t["proj"])
        ao = _attn(q, k, v, w["lamp"], w["subln"], lam_init, nbatch, seq, t["attn"])
        ch = _conv(u, w["conv_w"], w["conv_b"], w["conv_lg"], w["conv_lb"], nbatch, seq, t["conv"])
        of, ob = _gla(gq, gk, gv, gf, gb, nbatch, seq, t["gla"])
        x1, x1b, aff = _merge(x, ao, ch, of, ob, gr, w, t["merge"])
        pos_g, offs_g = [], []
        for n, tb, cap, sb in zip(group_tokens, token_bases, caps, slot_bases):
            pos, offs = _select(aff[:, tb:tb + n], cap, sb)
            pos_g.append(pos)
            offs_g.append(offs[:, :n // TOKEN_CHUNK])
        pos = jnp.concatenate(pos_g, axis=1)
        blo, bhi = _tile_bounds(offs_g, caps, chunk_bases, t["gather"])
        ye = _ffn(blo, bhi, pos.reshape(N_EXPERTS, 1, -1), aff.reshape(N_EXPERTS, 1, -1), x1b,
                  wgate, wup, wdown, l, cap_total, t["ffn"], t["gather"])
        starts, nwin = _window_bounds(offs_g, caps, slot_bases, t["combine"])
        split = group_tokens[0] // t["combine"] if l == depth - 1 else None
        res = _combine(starts, nwin, x1, pos.T, w["g2"], w["b2"], ye, t["combine"], split=split)
        x = res[0]
    return tuple(xo.reshape(xg.shape) for xo, xg in zip(res, xs))


def kernel(x_prompt, x_sample, w_in, da_lam_q1, da_lam_k1, da_lam_q2, da_lam_k2, da_subln_g, da_w_o, conv_w, conv_b, conv_ln_g, conv_ln_b, conv_w_o, gla_gate_w2, gla_gate_b, gla_norm_g, gla_w_o, w_out, ln1_g, ln1_b, w_router, w_gate, w_up, w_down, ln2_g, ln2_b):
    p = dict(w_in=w_in, da_lam_q1=da_lam_q1, da_lam_k1=da_lam_k1, da_lam_q2=da_lam_q2, da_lam_k2=da_lam_k2,
             da_subln_g=da_subln_g, da_w_o=da_w_o, conv_w=conv_w, conv_b=conv_b, conv_ln_g=conv_ln_g,
             conv_ln_b=conv_ln_b, conv_w_o=conv_w_o, gla_gate_w2=gla_gate_w2, gla_gate_b=gla_gate_b,
             gla_norm_g=gla_norm_g, gla_w_o=gla_w_o, w_out=w_out, ln1_g=ln1_g, ln1_b=ln1_b, w_router=w_router,
             w_gate=w_gate, w_up=w_up, w_down=w_down, ln2_g=ln2_g, ln2_b=ln2_b)
    return _encode([x_prompt, x_sample], p, DEPTH)
```

```python
import functools
import math

import jax
import jax.numpy as jnp
from jax import lax
from jax.experimental import pallas as pl
from jax.experimental.pallas import tpu as pltpu

F32 = jnp.float32
BF16 = jnp.bfloat16
I32 = jnp.int32

D_MODEL = 1024
DEPTH = 4
DA_HEADS = 4
DA_HEAD_DIM = 64
DA_WIDTH = DA_HEADS * 2 * DA_HEAD_DIM
ROPE_THETA = 10000.0
CONV_WIDTH = 512
CONV_KERNEL = 31
GLA_HEADS = 4
GLA_DK = 64
GLA_DV = 128
GLA_KEY_WIDTH = GLA_HEADS * GLA_DK
GLA_VAL_WIDTH = GLA_HEADS * GLA_DV
GLA_GATE_RANK = 16
GLA_TAU = 16.0
GLA_CHUNK = 64
N_BRANCH = 3
N_EXPERTS = 16
EXPERT_FF = 2048
CAPACITY_FACTOR = 2
DEEPNORM_ALPHA = (2 * DEPTH) ** 0.25
NORM_EPS = 1e-5
LOG2E = 1.4426950408889634

LANES = 128
TOKEN_CHUNK = 256
SLOT_WINDOW = 128
SLOT_ALIGN = 16
CONV_HALO = 16
VMEM_LIMIT = 56 * 1024 * 1024

_NT = (((1,), (1,)), ((), ()))
_TN = (((0,), (0,)), ((), ()))


def _cparams(*sem):
    return pltpu.CompilerParams(dimension_semantics=sem, vmem_limit_bytes=VMEM_LIMIT)


def _dot(a, b):
    return jnp.dot(a, b, preferred_element_type=F32)


def _sigmoid(x):
    return 1.0 / (1.0 + jnp.exp(-x))


def _layernorm(y, g, b):
    mu = jnp.mean(y, axis=-1, keepdims=True)
    yc = y - mu
    var = jnp.mean(yc * yc, axis=-1, keepdims=True)
    return yc * lax.rsqrt(var + NORM_EPS) * g + b


def _proj_kernel(x_ref, cos_ref, sin_ref, wqk_ref, wv_ref, wcu_ref, wg4_ref, wlr_ref, gw_ref, gb_ref,
                 q_ref, k_ref, v_ref, u_ref, gq_ref, gk_ref, gv_ref, gr_ref, gf_ref, gbw_ref):
    xb = x_ref[...].astype(BF16)
    tm = xb.shape[0]
    qk = _dot(xb, wqk_ref[...])
    cos = cos_ref[...]
    sin = sin_ref[...]
    lane = lax.broadcasted_iota(I32, (tm, LANES), 1)
    first_half = (lane % DA_HEAD_DIM) < (DA_HEAD_DIM // 2)
    nqb = DA_WIDTH // LANES
    for cb in range(2 * nqb):
        xc = qk[:, cb * LANES:(cb + 1) * LANES]
        rot = jnp.where(first_half, pltpu.roll(xc, LANES - DA_HEAD_DIM // 2, 1), pltpu.roll(xc, DA_HEAD_DIM // 2, 1))
        r = xc * cos + rot * sin
        if cb < nqb:
            q_ref[:, cb * LANES:(cb + 1) * LANES] = (r * (DA_HEAD_DIM ** -0.5 * LOG2E)).astype(BF16)
        else:
            k_ref[:, (cb - nqb) * LANES:(cb - nqb + 1) * LANES] = r.astype(BF16)
    v_ref[...] = _dot(xb, wv_ref[...]).astype(BF16)
    cu = _dot(xb, wcu_ref[...])
    u_ref[...] = cu[:, :CONV_WIDTH] * _sigmoid(cu[:, CONV_WIDTH:])
    g4 = _dot(xb, wg4_ref[...])
    kw, vw = GLA_KEY_WIDTH, GLA_VAL_WIDTH
    gq_ref[...] = g4[:, :kw] * (GLA_DK ** -0.5)
    gk_ref[...] = g4[:, kw:2 * kw]
    gv_ref[...] = g4[:, 2 * kw:2 * kw + vw]
    gr_ref[...] = g4[:, 2 * kw + vw:]
    lr = _dot(xb, wlr_ref[...])
    gp = _dot(lr.astype(BF16), gw_ref[...]) + gb_ref[...]
    lsig = jnp.minimum(gp, 0.0) - jnp.log(1.0 + jnp.exp(-jnp.abs(gp)))
    lsig = lsig * (1.0 / GLA_TAU)
    gf_ref[...] = lsig[:, :kw]
    gbw_ref[...] = lsig[:, kw:]


def _proj(x, cos, sin, w, seq, tm):
    nt = x.shape[0]
    nseq = seq // tm
    row = lambda i: (i, 0)
    const = lambda i: (0, 0)
    full = lambda a: pl.BlockSpec(a.shape, const)
    outs = [(DA_WIDTH, BF16), (DA_WIDTH, BF16), (DA_WIDTH, BF16), (CONV_WIDTH, F32),
            (GLA_KEY_WIDTH, F32), (GLA_KEY_WIDTH, F32), (GLA_VAL_WIDTH, F32), (GLA_VAL_WIDTH, F32),
            (GLA_KEY_WIDTH, F32), (GLA_KEY_WIDTH, F32)]
    return pl.pallas_call(
        _proj_kernel,
        grid=(nt // tm,),
        in_specs=[pl.BlockSpec((tm, D_MODEL), row),
                  pl.BlockSpec((tm, LANES), lambda i: (i % nseq, 0)),
                  pl.BlockSpec((tm, LANES), lambda i: (i % nseq, 0)),
                  full(w["wqk"]), full(w["wv"]), full(w["wcu"]), full(w["wg4"]), full(w["wlr"]),
                  full(w["gw"]), full(w["gb"])],
        out_specs=[pl.BlockSpec((tm, c), row) for c, _ in outs],
        out_shape=[jax.ShapeDtypeStruct((nt, c), d) for c, d in outs],
        compiler_params=_cparams("parallel"),
        name="proj",
    )(x, cos, sin, w["wqk"], w["wv"], w["wcu"], w["wg4"], w["wlr"], w["gw"], w["gb"])


def _attn_kernel(lamp_ref, g_ref, q_ref, k_ref, v_ref, o_ref, *, lam_init):
    lp = lamp_ref[...]
    lam = (jnp.exp(jnp.sum(lp[0:1] * lp[1:2], axis=1, keepdims=True))
           - jnp.exp(jnp.sum(lp[2:3] * lp[3:4], axis=1, keepdims=True)) + lam_init)
    for h in range(DA_HEADS):
        cols = slice(h * LANES, (h + 1) * LANES)
        q = q_ref[:, cols]
        k = k_ref[:, cols]
        lane = lax.broadcasted_iota(I32, q.shape, 1)
        zero = jnp.zeros_like(q)
        s0 = lax.dot_general(jnp.where(lane < DA_HEAD_DIM, q, zero), k, _NT, preferred_element_type=F32)
        s1 = lax.dot_general(jnp.where(lane >= DA_HEAD_DIM, q, zero), k, _NT, preferred_element_type=F32)
        e0 = jnp.exp2(s0 - jnp.max(s0, axis=1, keepdims=True))
        e1 = jnp.exp2(s1 - jnp.max(s1, axis=1, keepdims=True))
        r0 = 1.0 / jnp.sum(e0, axis=1, keepdims=True)
        r1 = lam / jnp.sum(e1, axis=1, keepdims=True)
        p = e0 * r0 - e1 * r1
        o = _dot(p.astype(BF16), v_ref[:, cols])
        ms = jnp.mean(o * o, axis=1, keepdims=True)
        y = o * lax.rsqrt(ms + NORM_EPS) * g_ref[...] * (1.0 - lam_init)
        o_ref[:, cols] = y.astype(BF16)


def _attn(q, k, v, lamp, subln_g, lam_init, nbatch, seq, tq):
    nt = q.shape[0]
    nq = seq // tq
    return pl.pallas_call(
        functools.partial(_attn_kernel, lam_init=lam_init),
        grid=(nbatch, nq),
        in_specs=[pl.BlockSpec(lamp.shape, lambda b, i: (0, 0)),
                  pl.BlockSpec(subln_g.shape, lambda b, i: (0, 0)),
                  pl.BlockSpec((tq, DA_WIDTH), lambda b, i: (b * nq + i, 0)),
                  pl.BlockSpec((seq, DA_WIDTH), lambda b, i: (b, 0)),
                  pl.BlockSpec((seq, DA_WIDTH), lambda b, i: (b, 0))],
        out_specs=pl.BlockSpec((tq, DA_WIDTH), lambda b, i: (b * nq + i, 0)),
        out_shape=jax.ShapeDtypeStruct((nt, DA_WIDTH), BF16),
        compiler_params=_cparams("parallel", "parallel"),
        name="diff_attn",
    )(lamp, subln_g, q, k, v)


CONV_ROWS = 32


SUBLANES = 8


def _conv_kernel(prev_ref, cur_ref, next_ref, w_ref, cb_ref, lg_ref, lb_ref, o_ref, scr, shifted, *, ts, nblk):
    i = pl.program_id(1)
    scr[0:CONV_HALO, :] = jnp.where(i > 0, prev_ref[...], 0.0)
    scr[CONV_HALO:CONV_HALO + ts, :] = cur_ref[...]
    scr[CONV_HALO + ts:2 * CONV_HALO + ts, :] = jnp.where(i < nblk - 1, next_ref[...], 0.0)
    span = ts + 2 * CONV_HALO - SUBLANES
    for s in range(1, SUBLANES):
        shifted[s - 1] = scr[s:s + span, :]
    w = w_ref[...]
    base = CONV_HALO - CONV_KERNEL // 2
    for rb in range(ts // CONV_ROWS):
        r0 = rb * CONV_ROWS
        acc = jnp.zeros((CONV_ROWS, CONV_WIDTH), F32)
        for t in range(CONV_KERNEL):
            phase = (base + t) % SUBLANES
            row = r0 + base + t - phase
            if phase == 0:
                xs = scr[row:row + CONV_ROWS, :]
            else:
                xs = shifted[phase - 1, row:row + CONV_ROWS, :]
            acc = acc + w[t:t + 1, :] * xs
        h = _layernorm(acc + cb_ref[...], lg_ref[...], lb_ref[...])
        o_ref[r0:r0 + CONV_ROWS, :] = (h * _sigmoid(h)).astype(BF16)


def _conv(u, w, cb, lg, lb, nbatch, seq, ts):
    nt = u.shape[0]
    nblk = seq // ts
    hb = ts // CONV_HALO
    nhalo = nt // CONV_HALO
    const = lambda b, i: (0, 0)
    return pl.pallas_call(
        functools.partial(_conv_kernel, ts=ts, nblk=nblk),
        grid=(nbatch, nblk),
        in_specs=[pl.BlockSpec((CONV_HALO, CONV_WIDTH), lambda b, i: (jnp.maximum((b * nblk + i) * hb - 1, 0), 0)),
                  pl.BlockSpec((ts, CONV_WIDTH), lambda b, i: (b * nblk + i, 0)),
                  pl.BlockSpec((CONV_HALO, CONV_WIDTH),
                               lambda b, i: (jnp.minimum((b * nblk + i + 1) * hb, nhalo - 1), 0)),
                  pl.BlockSpec(w.shape, const), pl.BlockSpec(cb.shape, const),
                  pl.BlockSpec(lg.shape, const), pl.BlockSpec(lb.shape, const)],
        out_specs=pl.BlockSpec((ts, CONV_WIDTH), lambda b, i: (b * nblk + i, 0)),
        out_shape=jax.ShapeDtypeStruct((nt, CONV_WIDTH), BF16),
        scratch_shapes=[pltpu.VMEM((ts + 2 * CONV_HALO, CONV_WIDTH), F32),
                        pltpu.VMEM((SUBLANES - 1, ts + 2 * CONV_HALO - SUBLANES, CONV_WIDTH), F32)],
        compiler_params=_cparams("parallel", "parallel"),
        name="conformer_conv",
    )(u, u, u, w, cb, lg, lb)


def _split3(g):
    hi = g.astype(BF16)
    r1 = g - hi.astype(F32)
    mid = r1.astype(BF16)
    lo = (r1 - mid.astype(F32)).astype(BF16)
    return hi, mid, lo


def _gla_kernel(qf_ref, kf_ref, vf_ref, gf_ref, qb_ref, kb_ref, vb_ref, gb_ref, of_ref, ob_ref, st_ref, *, tc):
    @pl.when(pl.program_id(1) == 0)
    def _():
        st_ref[...] = jnp.zeros_like(st_ref)

    c = GLA_CHUNK
    nchunk = tc // c
    npair = GLA_HEADS // 2
    refs = ((qf_ref, kf_ref, vf_ref, gf_ref, of_ref), (qb_ref, kb_ref, vb_ref, gb_ref, ob_ref))
    r = lax.broadcasted_iota(I32, (c, c), 0)
    cc = lax.broadcasted_iota(I32, (c, c), 1)
    r2 = lax.broadcasted_iota(I32, (c, 2 * c), 0)
    c2 = lax.broadcasted_iota(I32, (c, 2 * c), 1) % c
    keep = (cc <= r, cc >= r)
    keep2 = (c2 <= r2, c2 >= r2)
    tri = tuple(jnp.where(m, 1.0, 0.0).astype(BF16) for m in keep)
    head0 = lax.broadcasted_iota(I32, (c, LANES), 1) < GLA_DK
    vhead0 = lax.broadcasted_iota(I32, (c, 2 * GLA_DV), 1) < GLA_DV
    same_head = ((lax.broadcasted_iota(I32, (2 * GLA_DV, LANES), 0) < GLA_DV)
                 == (lax.broadcasted_iota(I32, (2 * GLA_DV, LANES), 1) < GLA_DK))
    items = [(d, p, s) for s in range(nchunk) for p in range(npair) for d in range(2)]

    def where_(d, p, s):
        ci = s if d == 0 else nchunk - 1 - s
        return slice(ci * c, (ci + 1) * c), slice(p * LANES, (p + 1) * LANES), slice(p * 2 * GLA_DV, (p + 1) * 2 * GLA_DV)

    b3 = {}
    for it in items:
        d, p, s = it
        rows, kl, _ = where_(*it)
        b3[it] = _dot(tri[d], jnp.concatenate(_split3(refs[d][3][rows, kl]), axis=1))
    qt, decay, amat, kvm, vbd = {}, {}, {}, {}, {}
    for it in items:
        d, p, s = it
        rows, kl, vl = where_(*it)
        b = b3[it][:, :LANES] + b3[it][:, LANES:2 * LANES] + b3[it][:, 2 * LANES:]
        b_last = b[0:1, :] if d == 1 else b[c - 1:c, :]
        q = refs[d][0][rows, kl]
        k = refs[d][1][rows, kl]
        qt[it] = (q * jnp.exp(b)).astype(BF16)
        kt = (k * jnp.exp(-b)).astype(BF16)
        kd = (k * jnp.exp(b_last - b)).astype(BF16)
        decay[it] = jnp.exp(b_last)
        zk = jnp.zeros_like(kt)
        kstack = jnp.concatenate([jnp.where(head0, kt, zk), jnp.where(head0, zk, kt)], axis=0)
        a = lax.dot_general(qt[it], kstack, _NT, preferred_element_type=F32)
        amat[it] = jnp.where(keep2[d], a, 0.0).astype(BF16)
        v = refs[d][2][rows, vl].astype(BF16)
        zv = jnp.zeros_like(v)
        vbd[it] = jnp.concatenate([jnp.where(vhead0, v, zv), jnp.where(vhead0, zv, v)], axis=0)
        kv = lax.dot_general(v, kd, _TN, preferred_element_type=F32)
        kvm[it] = jnp.where(same_head, kv, 0.0)
    pre = {}
    for d in range(2):
        for p in range(npair):
            st = st_ref[d, p]
            for s in range(nchunk):
                it = (d, p, s)
                pre[it] = st.astype(BF16)
                st = st * decay[it] + kvm[it]
            st_ref[d, p] = st
    for it in items:
        d, p, s = it
        rows, _, vl = where_(*it)
        refs[d][4][rows, vl] = (_dot(amat[it], vbd[it])
                                + lax.dot_general(qt[it], pre[it], _NT, preferred_element_type=F32))


def _gla(gq, gk, gv, gf, gb, nbatch, seq, tc):
    nt = gq.shape[0]
    nj = seq // tc
    fwd = lambda b, j: (b * nj + j, 0)
    bwd = lambda b, j: (b * nj + nj - 1 - j, 0)
    kspec = lambda m: pl.BlockSpec((tc, GLA_KEY_WIDTH), m)
    vspec = lambda m: pl.BlockSpec((tc, GLA_VAL_WIDTH), m)
    return pl.pallas_call(
        functools.partial(_gla_kernel, tc=tc),
        grid=(nbatch, nj),
        in_specs=[kspec(fwd), kspec(fwd), vspec(fwd), kspec(fwd), kspec(bwd), kspec(bwd), vspec(bwd), kspec(bwd)],
        out_specs=[vspec(fwd), vspec(bwd)],
        out_shape=[jax.ShapeDtypeStruct((nt, GLA_VAL_WIDTH), F32)] * 2,
        scratch_shapes=[pltpu.VMEM((2, GLA_HEADS // 2, 2 * GLA_DV, LANES), F32)],
        compiler_params=_cparams("parallel", "arbitrary"),
        name="bi_gla",
    )(gq, gk, gv, gf, gq, gk, gv, gb)


def _merge_kernel(x_ref, ao_ref, ch_ref, of_ref, ob_ref, gr_ref, ng_ref, wa_ref, wc_ref, wl_ref, wmg_ref, wout_ref,
                  g1_ref, b1_ref, wr_ref, x1_ref, x1b_ref, aff_ref):
    x = x_ref[...]
    xb = x.astype(BF16)
    a = _dot(ao_ref[...], wa_ref[...])
    c = _dot(ch_ref[...], wc_ref[...])
    o = of_ref[...] + ob_ref[...]
    r = gr_ref[...]
    ng = ng_ref[...]
    parts = []
    for h in range(GLA_HEADS):
        oh = o[:, h * GLA_DV:(h + 1) * GLA_DV]
        rh = r[:, h * GLA_DV:(h + 1) * GLA_DV]
        yh = oh * lax.rsqrt(jnp.mean(oh * oh, axis=1, keepdims=True) + NORM_EPS) * ng
        parts.append((yh * (rh * _sigmoid(rh))).astype(BF16))
    l = _dot(jnp.concatenate(parts, axis=1), wl_ref[...])
    m = _sigmoid(_dot(xb, wmg_ref[:, 0:D_MODEL])) * a
    m = m + _sigmoid(_dot(xb, wmg_ref[:, D_MODEL:2 * D_MODEL])) * c
    m = m + _sigmoid(_dot(xb, wmg_ref[:, 2 * D_MODEL:3 * D_MODEL])) * l
    y = DEEPNORM_ALPHA * x + _dot(m.astype(BF16), wout_ref[...])
    x1 = _layernorm(y, g1_ref[...], b1_ref[...])
    x1_ref[...] = x1
    x1b = x1.astype(BF16)
    x1b_ref[...] = x1b
    logits = lax.dot_general(wr_ref[...], x1b, _NT, preferred_element_type=F32)
    ex = jnp.exp(logits - jnp.max(logits, axis=0, keepdims=True))
    aff_ref[...] = ex / jnp.sum(ex, axis=0, keepdims=True)


def _merge(x, ao, ch, of, ob, gr, w, tm):
    nt = x.shape[0]
    row = lambda i: (i, 0)
    const = lambda i: (0, 0)
    full = lambda a: pl.BlockSpec(a.shape, const)
    names = ["ng", "wa", "wc", "wl", "wmg", "wout", "g1", "b1", "wr"]
    return pl.pallas_call(
        _merge_kernel,
        grid=(nt // tm,),
        in_specs=[pl.BlockSpec((tm, D_MODEL), row), pl.BlockSpec((tm, DA_WIDTH), row),
                  pl.BlockSpec((tm, CONV_WIDTH), row), pl.BlockSpec((tm, GLA_VAL_WIDTH), row),
                  pl.BlockSpec((tm, GLA_VAL_WIDTH), row), pl.BlockSpec((tm, GLA_VAL_WIDTH), row)]
                 + [full(w[n]) for n in names],
        out_specs=[pl.BlockSpec((tm, D_MODEL), row), pl.BlockSpec((tm, D_MODEL), row),
                   pl.BlockSpec((N_EXPERTS, tm), lambda i: (0, i))],
        out_shape=[jax.ShapeDtypeStruct((nt, D_MODEL), F32), jax.ShapeDtypeStruct((nt, D_MODEL), BF16),
                   jax.ShapeDtypeStruct((N_EXPERTS, nt), F32)],
        compiler_params=_cparams("parallel"),
        name="merge_ln1_router",
    )(x, ao, ch, of, ob, gr, *[w[n] for n in names])


def _select_kernel(aff_ref, pos_ref, offs_ref, sel_ref, *, cap, slot_base, idx_bits):
    ne, n = aff_ref.shape
    bits = pltpu.bitcast(aff_ref[...], I32)
    capf = float(cap)

    def count(mask):
        return jnp.sum(jnp.where(mask, 1.0, 0.0), axis=1, keepdims=True)

    def value_step(i, t):
        cand = t | jnp.left_shift(jnp.int32(1), 30 - i)
        return jnp.where(count(bits >= cand) >= capf, cand, t)

    thr = lax.fori_loop(0, 31, value_step, jnp.zeros((ne, 1), I32))
    above = bits > thr
    tie = bits == thr
    need = capf - count(above)
    idx = lax.broadcasted_iota(I32, (ne, n), 1)

    def index_step(i, ans):
        cand = ans | jnp.left_shift(jnp.int32(1), idx_bits - 1 - i)
        return jnp.where(count(tie & (idx < cand)) < need, cand, ans)

    last_tie = lax.fori_loop(0, idx_bits, index_step, jnp.zeros((ne, 1), I32))
    sel_ref[...] = jnp.where(above | (tie & (idx <= last_tie)), 1.0, 0.0)

    ch = TOKEN_CHUNK
    upper = jnp.where(lax.broadcasted_iota(I32, (ch, ch), 0) <= lax.broadcasted_iota(I32, (ch, ch), 1),
                      1.0, 0.0).astype(BF16)
    lane = lax.broadcasted_iota(I32, (ne, LANES), 1)

    offs_ref[...] = jnp.zeros_like(offs_ref)

    def chunk_step(j, run):
        off = pl.multiple_of(j * ch, ch)
        m = sel_ref[:, pl.ds(off, ch)]
        incl = _dot(m.astype(BF16), upper)
        p = jnp.where(m > 0.0, run + incl + (slot_base - 1.0), -1.0)
        pos_ref[:, pl.ds(off, ch)] = p.astype(I32)
        offs_ref[...] = jnp.where(lane == j, jnp.broadcast_to(run, (ne, LANES)).astype(I32), offs_ref[...])
        return run + incl[:, ch - 1:ch]

    lax.fori_loop(0, n // ch, chunk_step, jnp.zeros((ne, 1), F32))


def _select(aff, cap, slot_base):
    ne, n = aff.shape
    assert n % TOKEN_CHUNK == 0 and n // TOKEN_CHUNK <= LANES
    return pl.pallas_call(
        functools.partial(_select_kernel, cap=cap, slot_base=slot_base, idx_bits=max(1, (n - 1).bit_length())),
        out_shape=[jax.ShapeDtypeStruct((ne, n), I32), jax.ShapeDtypeStruct((ne, LANES), I32)],
        scratch_shapes=[pltpu.VMEM((ne, n), F32)],
        compiler_params=pltpu.CompilerParams(vmem_limit_bytes=VMEM_LIMIT),
        name="expert_select",
    )(aff)


GATHER_CHUNKS = 10
GATE_ROWS = 16


def _ffn_kernel(blo_ref, bhi_ref, pos_ref, aff_ref, x_hbm, wg_ref, wu_ref, wd_ref, ye_ref, xcat, sem, pcat, gcat,
                xe_ref, gate_ref, *, ts, sub):
    nj = pl.num_programs(1)
    j = pl.program_id(1)
    step = pl.program_id(0) * nj + j
    nsub = ts // sub
    nunits = pl.num_programs(0) * nj * nsub
    ch = TOKEN_CHUNK
    kb = GATHER_CHUNKS
    last_chunk = x_hbm.shape[0] // ch - 1

    def copy(c, which, k):
        return pltpu.make_async_copy(x_hbm.at[pl.ds(pl.multiple_of(c * ch, ch), ch), :],
                                     xcat.at[which, pl.ds(k * ch, ch), :], sem.at[which])

    def issue(lo, cnt, which):
        for k in range(kb):
            @pl.when(k < cnt)
            def _(k=k):
                copy(lo + k, which, k).start()

    def drain(cnt, which):
        for k in range(kb):
            @pl.when(k < cnt)
            def _(k=k):
                copy(0, which, k).wait()

    def unit_bounds(u):
        lo = blo_ref[u]
        return lo, bhi_ref[u] - lo

    @pl.when(step == 0)
    def _():
        xcat[...] = jnp.zeros_like(xcat)
        for u0 in range(2):
            lo, n = unit_bounds(u0)
            issue(lo, jnp.minimum(n, kb), u0)

    piece = lax.broadcasted_iota(I32, (GATE_ROWS, ch), 0)
    for t in range(nsub):
        u = step * nsub + t
        which = u % 2
        lo, n = unit_bounds(u)
        slot_id = lax.broadcasted_iota(I32, (sub, ch), 0) + (j * ts + t * sub)

        def build(first, cnt):
            for k in range(kb):
                c = jnp.minimum(first + k, last_chunk)
                off = pl.multiple_of(c * ch, ch)
                hit = (pos_ref[:, pl.ds(off, ch)] == slot_id) & (k < cnt)
                pcat[:, k * ch:(k + 1) * ch] = jnp.where(hit, 1.0, 0.0).astype(BF16)
                hi, mid, lo3 = _split3(aff_ref[:, pl.ds(off, ch)])
                gcat[:, k * ch:(k + 1) * ch] = jnp.where(
                    piece == 0, hi.astype(F32), jnp.where(piece == 1, mid.astype(F32),
                                                          jnp.where(piece == 2, lo3.astype(F32), 0.0))).astype(BF16)

        cnt0 = jnp.minimum(n, kb)
        build(lo, cnt0)
        drain(cnt0, which)
        rows = slice(t * sub, (t + 1) * sub)
        xe_ref[rows, :] = _dot(pcat[...], xcat[which])
        gate_ref[rows, :] = lax.dot_general(pcat[...], gcat[...], _NT, preferred_element_type=F32)

        def more(b, carry, which=which, lo=lo, n=n, rows=rows):
            first = lo + b * kb
            cnt = jnp.minimum(n - b * kb, kb)
            issue(first, cnt, which)
            build(first, cnt)
            drain(cnt, which)
            xe_ref[rows, :] += _dot(pcat[...], xcat[which])
            gate_ref[rows, :] += lax.dot_general(pcat[...], gcat[...], _NT, preferred_element_type=F32)
            return carry

        lax.fori_loop(1, (n + kb - 1) // kb, more, 0)

        @pl.when(u + 2 < nunits)
        def _(u=u, which=which):
            lo2, n2 = unit_bounds(u + 2)
            issue(lo2, jnp.minimum(n2, kb), which)

    xe = xe_ref[...].astype(BF16)
    g = _dot(xe, wg_ref[...])
    uu = _dot(xe, wu_ref[...])
    h = (g * _sigmoid(g)) * uu
    gate = jnp.sum(gate_ref[...], axis=1, keepdims=True)
    ye_ref[...] = (_dot(h.astype(BF16), wd_ref[...]) * gate).astype(BF16)


def _ffn(blo, bhi, pos3, aff3, x1b, wg, wu, wd, layer, cap_total, ts, sub):
    ntile = cap_total // ts
    nt = x1b.shape[0]
    per_expert = lambda e, j, *_: (e, 0, 0)
    layer_expert = lambda e, j, *_: (layer, e, 0, 0)
    return pl.pallas_call(
        functools.partial(_ffn_kernel, ts=ts, sub=sub),
        grid_spec=pltpu.PrefetchScalarGridSpec(
            num_scalar_prefetch=2,
            grid=(N_EXPERTS, ntile),
            in_specs=[pl.BlockSpec((None, 1, nt), per_expert),
                      pl.BlockSpec((None, 1, nt), per_expert),
                      pl.BlockSpec(memory_space=pl.ANY),
                      pl.BlockSpec((None, None, D_MODEL, EXPERT_FF), layer_expert),
                      pl.BlockSpec((None, None, D_MODEL, EXPERT_FF), layer_expert),
                      pl.BlockSpec((None, None, EXPERT_FF, D_MODEL), layer_expert)],
            out_specs=pl.BlockSpec((None, ts, D_MODEL), lambda e, j, *_: (e, j, 0)),
            scratch_shapes=[pltpu.VMEM((2, GATHER_CHUNKS * TOKEN_CHUNK, D_MODEL), BF16),
                            pltpu.SemaphoreType.DMA((2,)),
                            pltpu.VMEM((sub, GATHER_CHUNKS * TOKEN_CHUNK), BF16),
                            pltpu.VMEM((GATE_ROWS, GATHER_CHUNKS * TOKEN_CHUNK), BF16),
                            pltpu.VMEM((ts, D_MODEL), F32), pltpu.VMEM((ts, GATE_ROWS), F32)]),
        out_shape=jax.ShapeDtypeStruct((N_EXPERTS, cap_total, D_MODEL), BF16),
        compiler_params=_cparams("arbitrary", "arbitrary"),
        name="gather_expert_ffn",
    )(blo, bhi, pos3, aff3, x1b, wg, wu, wd)


def _combine_kernel(st_ref, nw_ref, x1_ref, post_ref, g2_ref, b2_ref, ye_hbm, *rest, cap_total, split):
    nout = 1 if split is None else 2
    out_refs, (ybuf, sem, xtra, sem_x, p_ref, acc_ref) = rest[:nout], rest[nout:]
    i = pl.program_id(0)
    ntiles = pl.num_programs(0)
    tm = x1_ref.shape[0]
    w = SLOT_WINDOW
    cur = i % 2

    def window(tile, e, k):
        lo = st_ref[tile * N_EXPERTS + e] + k * w
        return lo, pl.multiple_of(jnp.minimum(lo, cap_total - w), SLOT_ALIGN)

    def first_copy(tile, e, which):
        return pltpu.make_async_copy(ye_hbm.at[e, pl.ds(window(tile, e, 0)[1], w), :],
                                     ybuf.at[which, pl.ds(e * w, w), :], sem.at[which, e])

    @pl.when(i == 0)
    def _():
        for e in range(N_EXPERTS):
            first_copy(0, e, 0).start()

    @pl.when(i + 1 < ntiles)
    def _():
        for e in range(N_EXPERTS):
            first_copy(i + 1, e, 1 - cur).start()

    lane = lax.broadcasted_iota(I32, (tm, w), 1)

    def onehot(e, k):
        lo, start = window(i, e, k)
        pe = post_ref[:, e:e + 1]
        rel = jnp.where((pe >= lo) & (pe < start + w), pe - start, -1)
        return jnp.where(rel == lane, 1.0, 0.0).astype(BF16)

    for e in range(N_EXPERTS):
        p_ref[:, e * w:(e + 1) * w] = onehot(e, 0)
    for e in range(N_EXPERTS):
        first_copy(i, e, cur).wait()
    acc_ref[...] = _dot(p_ref[...], ybuf[cur])
    for e in range(N_EXPERTS):
        def extra(k, carry, e=e):
            cp = pltpu.make_async_copy(ye_hbm.at[e, pl.ds(window(i, e, k)[1], w), :], xtra, sem_x.at[0])
            cp.start()
            cp.wait()
            acc_ref[...] += _dot(onehot(e, k), xtra[...])
            return carry

        lax.fori_loop(1, nw_ref[i * N_EXPERTS + e], extra, 0)
    y = _layernorm(DEEPNORM_ALPHA * x1_ref[...] + acc_ref[...], g2_ref[...], b2_ref[...])
    if split is None:
        out_refs[0][...] = y
    else:
        @pl.when(i < split)
        def _():
            out_refs[0][...] = y

        @pl.when(i >= split)
        def _():
            out_refs[1][...] = y


def _combine(starts, nwin, x1, post, g2, b2, ye, tm, split=None):
    nt = x1.shape[0]
    ntiles = nt // tm
    cap_total = ye.shape[1]
    row = lambda i, *_: (i, 0)
    const = lambda i, *_: (0, 0)
    if split is None:
        out_specs = [pl.BlockSpec((tm, D_MODEL), row)]
        out_shape = [jax.ShapeDtypeStruct((nt, D_MODEL), F32)]
    else:
        out_specs = [pl.BlockSpec((tm, D_MODEL), lambda i, *_: (jnp.minimum(i, split - 1), 0)),
                     pl.BlockSpec((tm, D_MODEL), lambda i, *_: (jnp.maximum(i - split, 0), 0))]
        out_shape = [jax.ShapeDtypeStruct((split * tm, D_MODEL), F32),
                     jax.ShapeDtypeStruct(((ntiles - split) * tm, D_MODEL), F32)]
    return pl.pallas_call(
        functools.partial(_combine_kernel, cap_total=cap_total, split=split),
        grid_spec=pltpu.PrefetchScalarGridSpec(
            num_scalar_prefetch=2,
            grid=(ntiles,),
            in_specs=[pl.BlockSpec((tm, D_MODEL), row), pl.BlockSpec((tm, N_EXPERTS), row),
                      pl.BlockSpec(g2.shape, const), pl.BlockSpec(b2.shape, const),
                      pl.BlockSpec(memory_space=pl.ANY)],
            out_specs=out_specs,
            scratch_shapes=[pltpu.VMEM((2, N_EXPERTS * SLOT_WINDOW, D_MODEL), BF16),
                            pltpu.SemaphoreType.DMA((2, N_EXPERTS)),
                            pltpu.VMEM((SLOT_WINDOW, D_MODEL), BF16), pltpu.SemaphoreType.DMA((1,)),
                            pltpu.VMEM((tm, N_EXPERTS * SLOT_WINDOW), BF16), pltpu.VMEM((tm, D_MODEL), F32)]),
        out_shape=out_shape,
        compiler_params=_cparams("arbitrary"),
        name="combine_ln2",
    )(starts, nwin, x1, post, g2, b2, ye)


def _tile_bounds(offs_groups, caps, chunk_bases, ts):
    los, his = [], []
    for offs, cap, cbase in zip(offs_groups, caps, chunk_bases):
        nch = offs.shape[1]
        ends = jnp.concatenate([offs[:, 1:], jnp.full((N_EXPERTS, 1), cap, I32)], axis=1)
        s0 = (jnp.arange(cap // ts, dtype=I32) * ts)[None, :, None]
        los.append(cbase + jnp.sum((ends[:, None, :] <= s0).astype(I32), axis=2))
        his.append(cbase + jnp.sum((offs[:, None, :] < s0 + ts).astype(I32), axis=2))
        del nch
    return jnp.concatenate(los, axis=1).reshape(-1), jnp.concatenate(his, axis=1).reshape(-1)


def _window_bounds(offs_groups, caps, slot_bases, tm):
    starts, nwins = [], []
    per = tm // TOKEN_CHUNK
    for offs, cap, sbase in zip(offs_groups, caps, slot_bases):
        ends = jnp.concatenate([offs, jnp.full((N_EXPERTS, 1), cap, I32)], axis=1)
        first = ends[:, 0:-1:per] + sbase
        stop = ends[:, per::per] + sbase
        st = (first // SLOT_ALIGN) * SLOT_ALIGN
        nw = jnp.maximum((stop - st + SLOT_WINDOW - 1) // SLOT_WINDOW, 1)
        starts.append(st.T)
        nwins.append(nw.T)
    return jnp.concatenate(starts, axis=0).reshape(-1), jnp.concatenate(nwins, axis=0).reshape(-1)


def _prep_layer(l, p):
    bf = lambda a: a.astype(BF16)
    w_in = p["w_in"][l]
    c = 0
    cuts = {}
    for name, width in (("q", DA_WIDTH), ("k", DA_WIDTH), ("v", DA_WIDTH), ("cu", 2 * CONV_WIDTH),
                        ("g4", 2 * GLA_KEY_WIDTH + 2 * GLA_VAL_WIDTH), ("lr", 2 * GLA_GATE_RANK),
                        ("mg", N_BRANCH * D_MODEL)):
        cuts[name] = w_in[:, c:c + width]
        c += width
    gw2 = p["gla_gate_w2"][l]
    gw = jnp.zeros((LANES, 2 * GLA_KEY_WIDTH), F32)
    gw = gw.at[0:GLA_GATE_RANK, 0:GLA_KEY_WIDTH].set(gw2[0])
    gw = gw.at[GLA_GATE_RANK:2 * GLA_GATE_RANK, GLA_KEY_WIDTH:].set(gw2[1])
    lamp = jnp.zeros((8, LANES), F32)
    for r, nm in enumerate(("da_lam_q1", "da_lam_k1", "da_lam_q2", "da_lam_k2")):
        lamp = lamp.at[r, 0:DA_HEAD_DIM].set(p[nm][l])
    row = lambda a: a.reshape(1, -1)
    return {
        "wqk": bf(jnp.concatenate([cuts["q"], cuts["k"]], axis=1)), "wv": bf(cuts["v"]), "wcu": bf(cuts["cu"]),
        "wg4": bf(cuts["g4"]), "wlr": bf(jnp.pad(cuts["lr"], ((0, 0), (0, LANES - 2 * GLA_GATE_RANK)))),
        "gw": bf(gw), "gb": p["gla_gate_b"][l].reshape(1, -1),
        "lamp": lamp, "subln": row(p["da_subln_g"][l]),
        "conv_w": jnp.pad(p["conv_w"][l], ((0, 1), (0, 0))), "conv_b": row(p["conv_b"][l]),
        "conv_lg": row(p["conv_ln_g"][l]), "conv_lb": row(p["conv_ln_b"][l]),
        "ng": row(p["gla_norm_g"][l]), "wa": bf(p["da_w_o"][l]), "wc": bf(p["conv_w_o"][l]),
        "wl": bf(p["gla_w_o"][l]), "wmg": bf(cuts["mg"]), "wout": bf(p["w_out"][l]),
        "g1": row(p["ln1_g"][l]), "b1": row(p["ln1_b"][l]), "wr": bf(p["w_router"][l].T),
        "g2": row(p["ln2_g"][l]), "b2": row(p["ln2_b"][l]),
    }


def _rope_tables(seq):
    d = DA_HEAD_DIM
    inv = 1.0 / (ROPE_THETA ** (jnp.arange(0, d, 2, dtype=F32) / d))
    ang = jnp.arange(seq, dtype=F32)[:, None] * inv[None, :]
    c, s = jnp.cos(ang), jnp.sin(ang)
    cos = jnp.concatenate([c, c], axis=1)
    sin = jnp.concatenate([-s, s], axis=1)
    reps = LANES // d
    return jnp.tile(cos, (1, reps)), jnp.tile(sin, (1, reps))


def _tiles(seq, group_tokens, caps):
    g = functools.reduce(math.gcd, group_tokens)
    gc = functools.reduce(math.gcd, caps)
    return {
        "proj": min(512, seq), "attn": min(256, seq), "conv": min(256, seq), "gla": min(512, seq),
        "merge": min(512, seq), "ffn": min(512, gc), "gather": min(256, gc), "combine": min(512, g),
    }


def _encode(xs, p, depth):
    seq = xs[0].shape[1]
    nbatch = sum(x.shape[0] for x in xs)
    group_tokens = [x.shape[0] * seq for x in xs]
    caps = [CAPACITY_FACTOR * n // N_EXPERTS for n in group_tokens]
    slot_bases = [sum(caps[:i]) for i in range(len(caps))]
    token_bases = [sum(group_tokens[:i]) for i in range(len(caps))]
    chunk_bases = [t // TOKEN_CHUNK for t in token_bases]
    cap_total = sum(caps)
    t = _tiles(seq, group_tokens, caps)
    assert cap_total >= SLOT_WINDOW and all(c % t["ffn"] == 0 for c in caps)
    x = jnp.concatenate([x.reshape(-1, D_MODEL) for x in xs], axis=0)
    cos, sin = _rope_tables(seq)
    wgate, wup, wdown = (p[n].astype(BF16) for n in ("w_gate", "w_up", "w_down"))
    for l in range(depth):
        w = _prep_layer(l, p)
        lam_init = 0.8 - 0.6 * math.exp(-0.3 * l)
        q, k, v, u, gq, gk, gv, gr, gf, gb = _proj(x, cos, sin, w, seq, t["proj"])
        ao = _attn(q, k, v, w["lamp"], w["subln"], lam_init, nbatch, seq, t["attn"])
        ch = _conv(u, w["conv_w"], w["conv_b"], w["conv_lg"], w["conv_lb"], nbatch, seq, t["conv"])
        of, ob = _gla(gq, gk, gv, gf, gb, nbatch, seq, t["gla"])
        x1, x1b, aff = _merge(x, ao, ch, of, ob, gr, w, t["merge"])
        pos_g, offs_g = [], []
        for n, tb, cap, sb in zip(group_tokens, token_bases, caps, slot_bases):
            pos, offs = _select(aff[:, tb:tb + n], cap, sb)
            pos_g.append(pos)
            offs_g.append(offs[:, :n // TOKEN_CHUNK])
        pos = jnp.concatenate(pos_g, axis=1)
        blo, bhi = _tile_bounds(offs_g, caps, chunk_bases, t["gather"])
        ye = _ffn(blo, bhi, pos.reshape(N_EXPERTS, 1, -1), aff.reshape(N_EXPERTS, 1, -1), x1b,
                  wgate, wup, wdown, l, cap_total, t["ffn"], t["gather"])
        starts, nwin = _window_bounds(offs_g, caps, slot_bases, t["combine"])
        split = group_tokens[0] // t["combine"] if l == depth - 1 else None
        res = _combine(starts, nwin, x1, pos.T, w["g2"], w["b2"], ye, t["combine"], split=split)
        x = res[0]
    return tuple(xo.reshape(xg.shape) for xo, xg in zip(res, xs))


def kernel(x_prompt, x_sample, w_in, da_lam_q1, da_lam_k1, da_lam_q2, da_lam_k2, da_subln_g, da_w_o, conv_w, conv_b, conv_ln_g, conv_ln_b, conv_w_o, gla_gate_w2, gla_gate_b, gla_norm_g, gla_w_o, w_out, ln1_g, ln1_b, w_router, w_gate, w_up, w_down, ln2_g, ln2_b):
    p = dict(w_in=w_in, da_lam_q1=da_lam_q1, da_lam_k1=da_lam_k1, da_lam_q2=da_lam_q2, da_lam_k2=da_lam_k2,
             da_subln_g=da_subln_g, da_w_o=da_w_o, conv_w=conv_w, conv_b=conv_b, conv_ln_g=conv_ln_g,
             conv_ln_b=conv_ln_b, conv_w_o=conv_w_o, gla_gate_w2=gla_gate_w2, gla_gate_b=gla_gate_b,
             gla_norm_g=gla_norm_g, gla_w_o=gla_w_o, w_out=w_out, ln1_g=ln1_g, ln1_b=ln1_b, w_router=w_router,
             w_gate=w_gate, w_up=w_up, w_down=w_down, ln2_g=ln2_g, ln2_b=ln2_b)
    return _encode([x_prompt, x_sample], p, DEPTH)
```
